```python
import math
import jax, jax.numpy as jnp
from jax import lax
import numpy as np

D_MODEL = 1024
BATCH = 8
SEQ = 4096
DEPTH = 4

N_MEM = 256
NSA_HEADS = 8
NSA_KV_GROUPS = 2
NSA_HEAD_DIM = 64
NSA_CMP_LEN = 32
NSA_CMP_STRIDE = 16
NSA_CMP_HIDDEN = 256
NSA_SEL_LEN = 64
NSA_SEL_TOPK = 16
NSA_WINDOW = 512
NSA_Q_BLOCK = 64
NSA_Q_W = NSA_HEADS * NSA_HEAD_DIM
NSA_KV_W = NSA_KV_GROUPS * NSA_HEAD_DIM
RWKV_HEADS = 4
RWKV_HEAD_DIM = 64
RWKV_W = RWKV_HEADS * RWKV_HEAD_DIM
RWKV_DECAY_RANK = 64
RWKV_ICLR_RANK = 64
RWKV_GATE_RANK = 128
RWKV_LN_EPS = 64e-5
RWKV_SPLIT_SIZES = (RWKV_W, RWKV_W, RWKV_W, RWKV_DECAY_RANK, RWKV_ICLR_RANK, RWKV_GATE_RANK)
RWKV_SHIFT_W = 3 * RWKV_W + RWKV_DECAY_RANK + RWKV_ICLR_RANK + RWKV_GATE_RANK
S5_GROUPS = 16
S5_GROUP_WIDTH = 16
S5_STATE = 64
S5_W = S5_GROUPS * S5_GROUP_WIDTH
XA_HEADS = 4
XA_HEAD_DIM = 64
XA_W = XA_HEADS * XA_HEAD_DIM
D_FF = -(-8 * D_MODEL // (3 * 256)) * 256
IN_SPLIT_SIZES = (NSA_Q_W, 6 * NSA_KV_W, 3 * NSA_HEADS, RWKV_SHIFT_W, S5_W, 3 * D_MODEL)
IN_WIDTH = sum(IN_SPLIT_SIZES)
ROPE_THETA = 10000.0
NORM_EPS = 1e-6
NEG_INF = -1e30
SEL_FORCE_SCORE = 1e9

kernel_name = 'hybrid_nsa_rwkv7_s5_gated_trunk'


def _split_last(z, sizes):
    offsets = []
    acc = 0
    for size in sizes[:-1]:
        acc += size
        offsets.append(acc)
    return jnp.split(z, offsets, axis=-1)


def rmsnorm(x, g):
    xf = x.astype(jnp.float32)
    y = xf * lax.rsqrt(jnp.mean(xf * xf, axis=-1, keepdims=True) + NORM_EPS)
    return (y * g.astype(jnp.float32)).astype(x.dtype)


def rope_tables(seq, dim):
    inv = 1.0 / (ROPE_THETA ** (jnp.arange(0, dim, 2, dtype=jnp.float32) / dim))
    ang = jnp.arange(seq, dtype=jnp.float32)[:, None] * inv[None, :]
    return jnp.cos(ang), jnp.sin(ang)


def apply_rope(x, cos, sin):
    x1, x2 = jnp.split(x, 2, axis=-1)
    c = cos[None, :, None, :].astype(x.dtype)
    s = sin[None, :, None, :].astype(x.dtype)
    return jnp.concatenate([x1 * c - x2 * s, x1 * s + x2 * c], axis=-1)


def _compress_blocks(t, pos, w1, w2):
    B, S, G, hd = t.shape
    r = NSA_CMP_LEN // NSA_CMP_STRIDE
    chunks = t.reshape(B, S // NSA_CMP_STRIDE, NSA_CMP_STRIDE, G, hd)
    n_cmp = chunks.shape[1] - r + 1
    blocks = jnp.concatenate([chunks[:, i:i + n_cmp] for i in range(r)], axis=2)
    blocks = blocks + pos[:, None, :]
    flat = blocks.transpose(0, 1, 3, 2, 4).reshape(B, n_cmp, G, NSA_CMP_LEN * hd)
    return jax.nn.gelu(flat @ w1) @ w2


def nsa_attention(q, k_cmp, v_cmp, k_sel, v_sel, k_win, v_win, gates,
                  pos_k, pos_v, ck_w1, ck_w2, cv_w1, cv_w2, cos, sin):
    B, S, H, hd = q.shape
    G = NSA_KV_GROUPS
    R = H // G
    L = NSA_SEL_LEN
    QB = NSA_Q_BLOCK
    W = NSA_WINDOW
    n_sel = S // L
    top_k = min(NSA_SEL_TOPK, n_sel)
    scale = hd ** -0.5

    kc = _compress_blocks(k_cmp, pos_k, ck_w1, ck_w2)
    vc = _compress_blocks(v_cmp, pos_v, cv_w1, cv_w2)
    n_cmp = kc.shape[1]
    cmp_start = np.arange(n_cmp) * NSA_CMP_STRIDE
    sel_start = np.arange(n_sel) * L
    overlap = jnp.asarray(((cmp_start[:, None] < sel_start[None, :] + L) &
                           (cmp_start[:, None] + NSA_CMP_LEN > sel_start[None, :])).astype(np.float32))
    cmp_end = jnp.arange(n_cmp) * NSA_CMP_STRIDE + (NSA_CMP_LEN - 1)

    q_grp = q.reshape(B, S, G, R, hd)
    q_rot = apply_rope(q, cos, sin).reshape(B, S, G, R, hd)
    ks_blocks = apply_rope(k_sel, cos, sin).reshape(B, n_sel, L, G, hd).transpose(0, 3, 1, 2, 4)
    vs_blocks = v_sel.reshape(B, n_sel, L, G, hd).transpose(0, 3, 1, 2, 4)
    pad = ((0, 0), (W, 0), (0, 0), (0, 0))
    kw_pad = jnp.pad(apply_rope(k_win, cos, sin), pad)
    vw_pad = jnp.pad(v_win, pad)
    gates = gates.reshape(B, S, G, R, 3)
    b_ix = jnp.arange(B)[:, None, None]
    g_ix = jnp.arange(G)[None, :, None]
    blk = jnp.arange(n_sel)
    tok = jnp.arange(L)

    def query_block(c):
        t0 = c * QB
        t = t0 + jnp.arange(QB)
        qc = lax.dynamic_slice_in_dim(q_grp, t0, QB, axis=1)
        qr = lax.dynamic_slice_in_dim(q_rot, t0, QB, axis=1)
        gc = lax.dynamic_slice_in_dim(gates, t0, QB, axis=1)
        s = jnp.einsum('bqgrd,bngd->bgrqn', qc, kc).astype(jnp.float32) * scale
        ok = cmp_end[None, :] <= t[:, None]
        p_cmp = jax.nn.softmax(jnp.where(ok, s, NEG_INF), axis=-1) * (t >= NSA_CMP_LEN - 1)[:, None]
        o_cmp = jnp.einsum('bgrqn,bngd->bqgrd', p_cmp.astype(vc.dtype), vc)
        imp = jnp.einsum('bgrqn,nj->bgqj', p_cmp, overlap)
        cur = (t // L)[:, None]
        forced = (blk == 0) | (blk == cur) | (blk == cur - 1)
        imp = jnp.where(blk <= cur, jnp.where(forced, SEL_FORCE_SCORE, imp), NEG_INF)
        _, idx = lax.top_k(imp, top_k)
        idx_flat = idx.reshape(B, G, QB * top_k)
        ks = ks_blocks[b_ix, g_ix, idx_flat].reshape(B, G, QB, top_k, L, hd)
        vs = vs_blocks[b_ix, g_ix, idx_flat].reshape(B, G, QB, top_k, L, hd)
        key_pos = idx[..., None] * L + tok
        ok = key_pos <= t[:, None, None]
        s = jnp.einsum('bqgrd,bgqkld->bgrqkl', qr, ks).astype(jnp.float32) * scale
        s = jnp.where(ok[:, :, None], s, NEG_INF).reshape(B, G, R, QB, top_k * L)
        p = jax.nn.softmax(s, axis=-1).reshape(B, G, R, QB, top_k, L).astype(vs.dtype)
        o_sel = jnp.einsum('bgrqkl,bgqkld->bqgrd', p, vs)
        kw = lax.dynamic_slice_in_dim(kw_pad, t0, QB + W, axis=1)
        vw = lax.dynamic_slice_in_dim(vw_pad, t0, QB + W, axis=1)
        pos = t0 - W + jnp.arange(QB + W)
        ok = (pos[None, :] <= t[:, None]) & (pos[None, :] > t[:, None] - W) & (pos[None, :] >= 0)
        s = jnp.einsum('bqgrd,bkgd->bgrqk', qr, kw).astype(jnp.float32) * scale
        p = jax.nn.softmax(jnp.where(ok, s, NEG_INF), axis=-1).astype(vw.dtype)
        o_win = jnp.einsum('bgrqk,bkgd->bqgrd', p, vw)
        o = gc[..., 0:1] * o_cmp + gc[..., 1:2] * o_sel + gc[..., 2:3] * o_win
        return o.reshape(B, QB, H * hd)

    out = lax.map(query_block, jnp.arange(S // QB))
    return out.transpose(1, 0, 2, 3).reshape(B, S, H * hd)


def rwkv7_time_mix(z, mu, w0, w2, a0, a2, g2, k_k, k_a, r_k, ln_w, ln_b):
    B, S, _ = z.shape
    H, N = RWKV_HEADS, RWKV_HEAD_DIM
    z_prev = jnp.pad(z, ((0, 0), (1, 0), (0, 0)))[:, :-1]
    z = z + (z_prev - z) * mu
    r, k, v, w_lr, a_lr, g_lr = _split_last(z, RWKV_SPLIT_SIZES)
    w = -jax.nn.softplus(-(w0 + jnp.tanh(w_lr) @ w2)) - 0.5
    decay = jnp.exp(-jnp.exp(w.astype(jnp.float32)))
    a = jax.nn.sigmoid(a0 + a_lr @ a2)
    g = jax.nn.sigmoid(g_lr) @ g2

    def heads(t):
        return t.astype(jnp.float32).reshape(B, S, H, N)

    kk = heads(k * k_k)
    kk = kk / jnp.maximum(jnp.linalg.norm(kk, axis=-1, keepdims=True), 1e-12)
    k = k * (1.0 + (a - 1.0) * k_a)
    r, k, v, a, decay = heads(r), heads(k), heads(v), heads(a), heads(decay)

    def step(state, inp):
        r_t, k_t, v_t, kk_t, a_t, w_t = inp
        sa = jnp.einsum('bhvk,bhk->bhv', state, -kk_t)
        state = (state * w_t[:, :, None, :]
                 + sa[..., None] * (kk_t * a_t)[:, :, None, :]
                 + v_t[..., None] * k_t[:, :, None, :])
        return state, jnp.einsum('bhvk,bhk->bhv', state, r_t)

    xs = tuple(jnp.swapaxes(t, 0, 1) for t in (r, k, v, kk, a, decay))
    _, y = lax.scan(step, jnp.zeros((B, H, N, N), jnp.float32), xs)
    y = jnp.swapaxes(y, 0, 1)
    mean = jnp.mean(y, axis=-1, keepdims=True)
    var = jnp.mean(jnp.square(y - mean), axis=-1, keepdims=True)
    yn = ((y - mean) * lax.rsqrt(var + RWKV_LN_EPS)).reshape(B, S, RWKV_W) * ln_w + ln_b
    bonus = (jnp.sum(r * k * r_k.astype(jnp.float32), axis=-1, keepdims=True) * v).reshape(B, S, RWKV_W)
    return ((yn + bonus) * g).astype(z.dtype)


def _complex_affine_combine(e1, e2):
    a1r, a1i, b1r, b1i = e1
    a2r, a2i, b2r, b2i = e2
    ar = a1r * a2r - a1i * a2i
    ai = a1r * a2i + a1i * a2r
    br = a2r * b1r - a2i * b1i + b2r
    bi = a2r * b1i + a2i * b1r + b2i
    return ar, ai, br, bi


def s5_layer(u, lam_re, lam_im, log_dt, b_re, b_im, c_re, c_im, d, w_glu):
    B, S, _ = u.shape
    uf = u.astype(jnp.float32).reshape(B, S, S5_GROUPS, S5_GROUP_WIDTH)
    dt = jnp.exp(log_dt.astype(jnp.float32))[:, None]
    lr = lam_re.astype(jnp.float32)
    li = lam_im.astype(jnp.float32)
    mag = jnp.exp(lr * dt)
    abar_re = mag * jnp.cos(li * dt)
    abar_im = mag * jnp.sin(li * dt)
    den = lr * lr + li * li
    coef_re = ((abar_re - 1.0) * lr + abar_im * li) / den
    coef_im = (abar_im * lr - (abar_re - 1.0) * li) / den
    bu_re = jnp.einsum('bsgh,gph->bsgp', uf, b_re.astype(jnp.float32))
    bu_im = jnp.einsum('bsgh,gph->bsgp', uf, b_im.astype(jnp.float32))
    in_re = coef_re * bu_re - coef_im * bu_im
    in_im = coef_re * bu_im + coef_im * bu_re
    a_re = jnp.broadcast_to(abar_re, (1, S) + abar_re.shape)
    a_im = jnp.broadcast_to(abar_im, (1, S) + abar_im.shape)
    _, _, x_re, x_im = lax.associative_scan(_complex_affine_combine, (a_re, a_im, in_re, in_im), axis=1)
    y = (jnp.einsum('bsgp,ghp->bsgh', x_re, c_re.astype(jnp.float32))
         - jnp.einsum('bsgp,ghp->bsgh', x_im, c_im.astype(jnp.float32))
         + d.astype(jnp.float32) * uf)
    y = jax.nn.gelu(y.reshape(B, S, S5_W))
    y = y * jax.nn.sigmoid(y @ w_glu.astype(jnp.float32))
    return y.astype(u.dtype)


def hybrid_mixer(h, w_in, nsa_cmp_pos_k, nsa_cmp_pos_v, nsa_ck_w1, nsa_ck_w2, nsa_cv_w1, nsa_cv_w2,
                 rwkv_mu, rwkv_w0, rwkv_w2, rwkv_a0, rwkv_a2, rwkv_g2, rwkv_k_k, rwkv_k_a, rwkv_r_k,
                 rwkv_ln_w, rwkv_ln_b, s5_lam_re, s5_lam_im, s5_log_dt, s5_b_re, s5_b_im, s5_c_re,
                 s5_c_im, s5_d, s5_w_glu, w_up_nsa, w_up_rwkv, w_up_s5, w_out, cos, sin):
    B, S, _ = h.shape
    z = h @ w_in
    z_q, z_kv, z_ng, z_rwkv, z_s5, z_mg = _split_last(z, IN_SPLIT_SIZES)
    q = z_q.reshape(B, S, NSA_HEADS, NSA_HEAD_DIM)
    kvs = [t.reshape(B, S, NSA_KV_GROUPS, NSA_HEAD_DIM) for t in jnp.split(z_kv, 6, axis=-1)]
    nsa_gates = jax.nn.sigmoid(z_ng).reshape(B, S, NSA_HEADS, 3)
    y_nsa = nsa_attention(q, kvs[0], kvs[1], kvs[2], kvs[3], kvs[4], kvs[5], nsa_gates,
                          nsa_cmp_pos_k, nsa_cmp_pos_v, nsa_ck_w1, nsa_ck_w2, nsa_cv_w1, nsa_cv_w2, cos, sin)
    y_rwkv = rwkv7_time_mix(z_rwkv, rwkv_mu, rwkv_w0, rwkv_w2, rwkv_a0, rwkv_a2, rwkv_g2,
                            rwkv_k_k, rwkv_k_a, rwkv_r_k, rwkv_ln_w, rwkv_ln_b)
    y_s5 = s5_layer(z_s5, s5_lam_re, s5_lam_im, s5_log_dt, s5_b_re, s5_b_im, s5_c_re, s5_c_im, s5_d, s5_w_glu)
    g_nsa, g_rwkv, g_s5 = jnp.split(jax.nn.sigmoid(z_mg), 3, axis=-1)
    merged = g_nsa * (y_nsa @ w_up_nsa) + g_rwkv * (y_rwkv @ w_up_rwkv) + g_s5 * (y_s5 @ w_up_s5)
    return merged @ w_out


def cross_attention(h, m, w_q, w_k, w_v, w_o):
    B, S, _ = h.shape
    M = m.shape[1]
    q = (h @ w_q).reshape(B, S, XA_HEADS, XA_HEAD_DIM)
    k = (m @ w_k).reshape(B, M, XA_HEADS, XA_HEAD_DIM)
    v = (m @ w_v).reshape(B, M, XA_HEADS, XA_HEAD_DIM)
    s = jnp.einsum('bshd,bmhd->bhsm', q, k).astype(jnp.float32) * (XA_HEAD_DIM ** -0.5)
    p = jax.nn.softmax(s, axis=-1).astype(v.dtype)
    return jnp.einsum('bhsm,bmhd->bshd', p, v).reshape(B, S, XA_W) @ w_o


def swiglu_ffn(h, w_gate, w_up, w_down):
    return (jax.nn.silu(h @ w_gate) * (h @ w_up)) @ w_down


def setup_inputs(seed: int = 0) -> dict:
    key = jax.random.key(seed)
    split = jax.random.split(key, 64)
    keys = iter([split[i] for i in range(64)])

    def normal(shape, scale):
        return jax.random.normal(next(keys), shape, jnp.float32) * scale

    def uniform(shape, lo, hi):
        return jax.random.uniform(next(keys), shape, jnp.float32, lo, hi)

    L = DEPTH
    hd = NSA_HEAD_DIM
    ramp = (jnp.arange(RWKV_W, dtype=jnp.float32) / (RWKV_W - 1)) ** 0.7
    lam_im_base = math.pi * jnp.arange(S5_STATE, dtype=jnp.float32)
    return {
        'x': normal((BATCH, SEQ, D_MODEL), 1.0),
        'mem': normal((BATCH, N_MEM, D_MODEL), 1.0),
        'norm_gains': 1.0 + normal((L, 6, D_MODEL), 0.05),
        'mem_norm': 1.0 + normal((L, D_MODEL), 0.05),
        'w_in': normal((L, D_MODEL, IN_WIDTH), D_MODEL ** -0.5),
        'nsa_cmp_pos_k': normal((L, NSA_CMP_LEN, hd), 0.02),
        'nsa_cmp_pos_v': normal((L, NSA_CMP_LEN, hd), 0.02),
        'nsa_ck_w1': normal((L, NSA_CMP_LEN * hd, NSA_CMP_HIDDEN), (NSA_CMP_LEN * hd) ** -0.5),
        'nsa_ck_w2': normal((L, NSA_CMP_HIDDEN, hd), NSA_CMP_HIDDEN ** -0.5),
        'nsa_cv_w1': normal((L, NSA_CMP_LEN * hd, NSA_CMP_HIDDEN), (NSA_CMP_LEN * hd) ** -0.5),
        'nsa_cv_w2': normal((L, NSA_CMP_HIDDEN, hd), NSA_CMP_HIDDEN ** -0.5),
        'rwkv_mu': uniform((L, RWKV_SHIFT_W), 0.0, 1.0),
        'rwkv_w0': -6.0 + 5.0 * ramp + normal((L, RWKV_W), 0.1),
        'rwkv_w2': normal((L, RWKV_DECAY_RANK, RWKV_W), 0.1 * RWKV_DECAY_RANK ** -0.5),
        'rwkv_a0': normal((L, RWKV_W), 0.1),
        'rwkv_a2': normal((L, RWKV_ICLR_RANK, RWKV_W), RWKV_ICLR_RANK ** -0.5),
        'rwkv_g2': normal((L, RWKV_GATE_RANK, RWKV_W), RWKV_GATE_RANK ** -0.5),
        'rwkv_k_k': 0.85 + normal((L, RWKV_W), 0.02),
        'rwkv_k_a': 1.0 + normal((L, RWKV_W), 0.02),
        'rwkv_r_k': normal((L, RWKV_HEADS, RWKV_HEAD_DIM), 0.1),
        'rwkv_ln_w': 1.0 + normal((L, RWKV_W), 0.05),
        'rwkv_ln_b': normal((L, RWKV_W), 0.02),
        's5_lam_re': -0.5 + normal((L, S5_GROUPS, S5_STATE), 0.01),
        's5_lam_im': lam_im_base + normal((L, S5_GROUPS, S5_STATE), 0.01),
        's5_log_dt': uniform((L, S5_GROUPS), math.log(1e-3), math.log(1e-1)),
        's5_b_re': normal((L, S5_GROUPS, S5_STATE, S5_GROUP_WIDTH), (2 * S5_GROUP_WIDTH) ** -0.5),
        's5_b_im': normal((L, S5_GROUPS, S5_STATE, S5_GROUP_WIDTH), (2 * S5_GROUP_WIDTH) ** -0.5),
        's5_c_re': normal((L, S5_GROUPS, S5_GROUP_WIDTH, S5_STATE), S5_STATE ** -0.5),
        's5_c_im': normal((L, S5_GROUPS, S5_GROUP_WIDTH, S5_STATE), S5_STATE ** -0.5),
        's5_d': normal((L, S5_GROUPS, S5_GROUP_WIDTH), 1.0),
        's5_w_glu': normal((L, S5_W, S5_W), S5_W ** -0.5),
        'w_up_nsa': normal((L, NSA_Q_W, D_MODEL), NSA_Q_W ** -0.5),
        'w_up_rwkv': normal((L, RWKV_W, D_MODEL), RWKV_W ** -0.5),
        'w_up_s5': normal((L, S5_W, D_MODEL), S5_W ** -0.5),
        'w_out': normal((L, D_MODEL, D_MODEL), D_MODEL ** -0.5),
        'xa_w_q': normal((L, D_MODEL, XA_W), D_MODEL ** -0.5),
        'xa_w_k': normal((L, D_MODEL, XA_W), D_MODEL ** -0.5),
        'xa_w_v': normal((L, D_MODEL, XA_W), D_MODEL ** -0.5),
        'xa_w_o': normal((L, XA_W, D_MODEL), XA_W ** -0.5),
        'ffn_w_gate': normal((L, D_MODEL, D_FF), D_MODEL ** -0.5),
        'ffn_w_up': normal((L, D_MODEL, D_FF), D_MODEL ** -0.5),
        'ffn_w_down': normal((L, D_FF, D_MODEL), D_FF ** -0.5),
    }


def reference(x, mem, norm_gains, mem_norm, w_in, nsa_cmp_pos_k, nsa_cmp_pos_v, nsa_ck_w1, nsa_ck_w2,
              nsa_cv_w1, nsa_cv_w2, rwkv_mu, rwkv_w0, rwkv_w2, rwkv_a0, rwkv_a2, rwkv_g2, rwkv_k_k,
              rwkv_k_a, rwkv_r_k, rwkv_ln_w, rwkv_ln_b, s5_lam_re, s5_lam_im, s5_log_dt, s5_b_re,
              s5_b_im, s5_c_re, s5_c_im, s5_d, s5_w_glu, w_up_nsa, w_up_rwkv, w_up_s5, w_out,
              xa_w_q, xa_w_k, xa_w_v, xa_w_o, ffn_w_gate, ffn_w_up, ffn_w_down):
    cos, sin = rope_tables(x.shape[1], NSA_HEAD_DIM)
    for l in range(DEPTH):
        h = rmsnorm(x, norm_gains[l, 0])
        y = hybrid_mixer(h, w_in[l], nsa_cmp_pos_k[l], nsa_cmp_pos_v[l], nsa_ck_w1[l], nsa_ck_w2[l],
                         nsa_cv_w1[l], nsa_cv_w2[l], rwkv_mu[l], rwkv_w0[l], rwkv_w2[l], rwkv_a0[l],
                         rwkv_a2[l], rwkv_g2[l], rwkv_k_k[l], rwkv_k_a[l], rwkv_r_k[l], rwkv_ln_w[l],
                         rwkv_ln_b[l], s5_lam_re[l], s5_lam_im[l], s5_log_dt[l], s5_b_re[l], s5_b_im[l],
                         s5_c_re[l], s5_c_im[l], s5_d[l], s5_w_glu[l], w_up_nsa[l], w_up_rwkv[l],
                         w_up_s5[l], w_out[l], cos, sin)
        x = x + rmsnorm(y, norm_gains[l, 1])
        h = rmsnorm(x, norm_gains[l, 2])
        m = rmsnorm(mem, mem_norm[l])
        x = x + rmsnorm(cross_attention(h, m, xa_w_q[l], xa_w_k[l], xa_w_v[l], xa_w_o[l]), norm_gains[l, 3])
        h = rmsnorm(x, norm_gains[l, 4])
        x = x + rmsnorm(swiglu_ffn(h, ffn_w_gate[l], ffn_w_up[l], ffn_w_down[l]), norm_gains[l, 5])
    return x
```

```python
import functools
import math

import jax
import jax.numpy as jnp
import numpy as np
from jax import lax
from jax.experimental import pallas as pl
from jax.experimental.pallas import tpu as pltpu

NSA_HEADS = 8
NSA_KV_GROUPS = 2
NSA_HEAD_DIM = 64
NSA_CMP_LEN = 32
NSA_CMP_STRIDE = 16
NSA_SEL_LEN = 64
NSA_SEL_TOPK = 16
NSA_WINDOW = 512
RWKV_HEADS = 4
RWKV_HEAD_DIM = 64
RWKV_LN_EPS = 64e-5
S5_GROUPS = 16
S5_GROUP_WIDTH = 16
S5_STATE = 64
XA_HEADS = 4
XA_HEAD_DIM = 64
ROPE_THETA = 10000.0
NORM_EPS = 1e-6
NEG_INF = -1e30
SEL_FORCE_SCORE = 1e9

LANES = 128
SUBLANES = 8
MXU_DTYPE = jnp.bfloat16
EXACT = lax.Precision.HIGHEST
VMEM_LIMIT = 56 << 20

F32 = jnp.float32


def _cparams(*sem):
    return pltpu.CompilerParams(dimension_semantics=sem, vmem_limit_bytes=VMEM_LIMIT)


def _const_spec(shape):
    nd = len(shape)
    return pl.BlockSpec(shape, lambda *_: (0,) * nd, pipeline_mode=pl.Buffered(1))


def _dot(a, b):
    return jnp.dot(a.astype(MXU_DTYPE), b.astype(MXU_DTYPE), preferred_element_type=F32)


def _dot_nt(a, b):
    return lax.dot_general(a.astype(MXU_DTYPE), b.astype(MXU_DTYPE), (((1,), (1,)), ((), ())),
                           preferred_element_type=F32)


def _dotx(a, b):
    return jnp.dot(a, b, preferred_element_type=F32, precision=EXACT)


def _dotx_nt(a, b):
    return lax.dot_general(a, b, (((1,), (1,)), ((), ())), preferred_element_type=F32, precision=EXACT)


def _rms(x, g):
    return x * lax.rsqrt(jnp.mean(x * x, axis=-1, keepdims=True) + NORM_EPS) * g


def _sigmoid(x):
    return 1.0 / (1.0 + jnp.exp(-x))


def _gelu(x):
    return 0.5 * x * (1.0 + jnp.tanh(math.sqrt(2.0 / math.pi) * (x + 0.044715 * (x * x * x))))


def _softmax_last(s):
    m = jnp.max(s, axis=-1, keepdims=True)
    e = jnp.exp(s - m)
    return e / jnp.sum(e, axis=-1, keepdims=True)


_SEG = dict(q=(0, 512), kc=(512, 640), vc=(640, 768), ks=(768, 1024), vs=(1024, 1280), kw=(1280, 1536),
            vw=(1536, 1792), ng=(1792, 1920), rw=(1920, 2944), s5=(2944, 3200), mg=(3200, 6272))
_W_IN_COLS = 6272


def _rope128(x, cos, sin_signed):
    lane = lax.broadcasted_iota(jnp.int32, x.shape, 1)
    first = (lane % NSA_HEAD_DIM) < (NSA_HEAD_DIM // 2)
    rot = jnp.where(first, pltpu.roll(x, LANES - NSA_HEAD_DIM // 2, 1), pltpu.roll(x, NSA_HEAD_DIM // 2, 1))
    return x * cos + rot * sin_signed


def _inproj_kernel(x_ref, g_ref, cos_ref, sin_ref, w_ref,
                   qc_o, qr_o, kc_o, vc_o, ks_o, vs_o, kw_o, vw_o, ng_o, rw_o, s5_o, mg_o):
    h = _rms(x_ref[...], g_ref[...]).astype(MXU_DTYPE)
    cos = cos_ref[...]
    sin = sin_ref[...]
    scale = NSA_HEAD_DIM ** -0.5

    def seg(name):
        a, b = _SEG[name]
        return jnp.dot(h, w_ref[:, a:b], preferred_element_type=F32)

    q = seg("q")
    qc_o[...] = (q * scale).astype(qc_o.dtype)
    for c in range(q.shape[1] // LANES):
        sl = slice(c * LANES, (c + 1) * LANES)
        qr_o[:, sl] = (_rope128(q[:, sl], cos, sin) * scale).astype(qr_o.dtype)
    kc_o[...] = seg("kc")
    vc_o[...] = seg("vc")
    for name, out in (("ks", ks_o), ("kw", kw_o)):
        k = seg(name)
        for c in range(2):
            sl = slice(c * LANES, (c + 1) * LANES)
            out[:, sl] = _rope128(k[:, sl], cos, sin).astype(out.dtype)
    vs_o[...] = seg("vs").astype(vs_o.dtype)
    vw_o[...] = seg("vw").astype(vw_o.dtype)
    ng_o[...] = _sigmoid(seg("ng"))
    rw_o[...] = seg("rw")
    s5_o[...] = seg("s5")
    mg_o[...] = _sigmoid(seg("mg"))


def _dup_groups(w):
    hd = NSA_HEAD_DIM
    return jnp.concatenate([w[:, :hd], w[:, :hd], w[:, hd:], w[:, hd:]], axis=1)


def _regroup_w_in(w):
    d = w.shape[0]
    q_w = NSA_HEADS * NSA_HEAD_DIM
    kv_w = NSA_KV_GROUPS * NSA_HEAD_DIM
    o = 0
    wq = w[:, o:o + q_w]; o += q_w
    kv = [w[:, o + i * kv_w:o + (i + 1) * kv_w] for i in range(6)]; o += 6 * kv_w
    wng = w[:, o:o + 3 * NSA_HEADS]; o += 3 * NSA_HEADS
    rw_w = 3 * RWKV_HEADS * RWKV_HEAD_DIM + 256
    wrw = w[:, o:o + rw_w]; o += rw_w
    s5_w = S5_GROUPS * S5_GROUP_WIDTH
    ws5 = w[:, o:o + s5_w]; o += s5_w
    wmg = w[:, o:]
    wng = jnp.pad(wng, ((0, 0), (0, LANES - wng.shape[1])))
    out = jnp.concatenate([wq, kv[0], kv[1], _dup_groups(kv[2]), _dup_groups(kv[3]), _dup_groups(kv[4]),
                           _dup_groups(kv[5]), wng, wrw, ws5, wmg], axis=1)
    assert out.shape == (d, _W_IN_COLS), out.shape
    return out.astype(MXU_DTYPE)


def _inproj(x2, gain, cos128, sin128, w_all, seq, tm):
    m, d = x2.shape
    n_tab = seq // tm
    row = lambda w: pl.BlockSpec((tm, w), lambda i: (i, 0))
    tab = pl.BlockSpec((tm, LANES), lambda i: (i % n_tab, 0))
    widths = dict(qc=512, qr=512, kc=128, vc=128, ks=256, vs=256, kw=256, vw=256, ng=128, rw=1024, s5=256, mg=3072)
    dtypes = dict(qc=MXU_DTYPE, qr=MXU_DTYPE, kc=F32, vc=F32, ks=MXU_DTYPE, vs=MXU_DTYPE, kw=MXU_DTYPE,
                  vw=MXU_DTYPE, ng=F32, rw=F32, s5=F32, mg=F32)
    names = list(widths)
    outs = pl.pallas_call(
        _inproj_kernel,
        grid=(m // tm,),
        in_specs=[row(d), _const_spec((1, d)), tab, tab, _const_spec(w_all.shape)],
        out_specs=[row(widths[n]) for n in names],
        out_shape=[jax.ShapeDtypeStruct((m, widths[n]), dtypes[n]) for n in names],
        compiler_params=_cparams("parallel"),
    )(x2, gain, cos128, sin128, w_all)
    return dict(zip(names, outs))


def _compress_kernel(k_ref, v_ref, pk_ref, pv_ref, kw1_ref, kw2_ref, vw1_ref, vw2_ref, kc_o, vc_o):
    def one(x_ref, p_ref, w1_ref, w2_ref, out):
        x = x_ref[0]
        n = x.shape[0]
        top = _dot(x + p_ref[0:1, :], w1_ref[0])
        bot = _dot(x + p_ref[1:2, :], w1_ref[1])
        hid = top + pltpu.roll(bot, n - 1, 0)
        out[0] = _dot(_gelu(hid), w2_ref[...]).astype(out.dtype)

    one(k_ref, pk_ref, kw1_ref, kw2_ref, kc_o)
    one(v_ref, pv_ref, vw1_ref, vw2_ref, vc_o)


def _compress_weights(pos, w1, w2):
    g, hd, half = NSA_KV_GROUPS, NSA_HEAD_DIM, NSA_CMP_STRIDE
    hidden = w1.shape[1]
    eye = jnp.eye(g, dtype=w1.dtype)
    w1r = w1.reshape(2, half, hd, hidden)
    w1e = jnp.einsum("pldj,ab->pladbj", w1r, eye).reshape(2, half * g * hd, g * hidden)
    w2d = jnp.concatenate([w2, w2], axis=1)
    w2e = jnp.einsum("jd,ab->ajbd", w2d, eye).reshape(g * hidden, g * 2 * hd)
    pos_e = jnp.broadcast_to(pos.reshape(2, half, 1, hd), (2, half, g, hd)).reshape(2, half * g * hd)
    return pos_e.astype(F32), w1e.astype(MXU_DTYPE), w2e.astype(MXU_DTYPE)


def _compress(kc, vc, pk, kw1, kw2, pv, vw1, vw2):
    b, n, width = kc.shape
    blk = pl.BlockSpec((1, n, width), lambda i: (i, 0, 0))
    out_w = kw2.shape[1]
    out_blk = pl.BlockSpec((1, n, out_w), lambda i: (i, 0, 0))
    return pl.pallas_call(
        _compress_kernel,
        grid=(b,),
        in_specs=[blk, blk, _const_spec(pk.shape), _const_spec(pv.shape), _const_spec(kw1.shape),
                  _const_spec(kw2.shape), _const_spec(vw1.shape), _const_spec(vw2.shape)],
        out_specs=[out_blk, out_blk],
        out_shape=[jax.ShapeDtypeStruct((b, n, out_w), MXU_DTYPE)] * 2,
        compiler_params=_cparams("parallel"),
    )(kc, vc, pk, pv, kw1, kw2, vw1, vw2)


NSA_TQ = 128
NSA_TK = 512


def _nsa_kernel(qc_ref, qr_ref, kc_ref, vc_ref, ks_ref, vs_ref, kw_ref, vw_ref, ng_ref,
                ovl_ref, exp_ref, gexp_ref, y_ref, *, top_k):
    tq = qc_ref.shape[1]
    seq = ks_ref.shape[1]
    n_cmp_pad = kc_ref.shape[1]
    heads_per_group = NSA_HEADS // NSA_KV_GROUPS
    rows = heads_per_group * tq
    t0 = pl.program_id(1) * tq
    tcol = t0 + lax.broadcasted_iota(jnp.int32, (tq, 1), 0)
    lane = lax.broadcasted_iota(jnp.int32, (tq, LANES), 1)
    low_half = lane < NSA_HEAD_DIM

    def stack_heads(q_ref, g):
        parts = []
        for r in range(heads_per_group):
            h = g * heads_per_group + r
            slab = q_ref[0, :, (h // 2) * LANES:(h // 2 + 1) * LANES]
            keep = low_half if h % 2 == 0 else jnp.logical_not(low_half)
            parts.append(jnp.where(keep, slab, jnp.zeros_like(slab)))
        return jnp.concatenate(parts, axis=0)

    def unstack(o, g, y_parts):
        for pair in range(heads_per_group // 2):
            even = o[(2 * pair) * tq:(2 * pair + 1) * tq]
            odd = o[(2 * pair + 1) * tq:(2 * pair + 2) * tq]
            y_parts[g * (heads_per_group // 2) + pair] = jnp.where(low_half, even, odd)

    blk_row = lax.broadcasted_iota(jnp.int32, (LANES, tq), 0)
    n_sel = seq // NSA_SEL_LEN
    cmp_end = lax.broadcasted_iota(jnp.int32, (1, n_cmp_pad), 1) * NSA_CMP_STRIDE + (NSA_CMP_LEN - 1)
    cmp_ok = cmp_end <= tcol
    cmp_live = (tcol >= NSA_CMP_LEN - 1).astype(F32)
    cur_row = (t0 + lax.broadcasted_iota(jnp.int32, (LANES, tq), 1)) // NSA_SEL_LEN

    y_cmp, y_sel, y_win = [None] * 4, [None] * 4, [None] * 4
    for g in range(NSA_KV_GROUPS):
        gl = slice(g * LANES, (g + 1) * LANES)
        qc = stack_heads(qc_ref, g)
        qr = stack_heads(qr_ref, g)

        s = _dot_nt(qc, kc_ref[0, :, gl]).reshape(heads_per_group, tq, n_cmp_pad)
        p = _softmax_last(jnp.where(cmp_ok[None], s, NEG_INF)) * cmp_live[None]
        unstack(_dot(p.reshape(rows, n_cmp_pad), vc_ref[0, :, gl]), g, y_cmp)

        imp = _dotx(jnp.sum(p, axis=0), ovl_ref[...])
        imp_t = imp.T
        forced = (blk_row == 0) | (blk_row == cur_row) | (blk_row == cur_row - 1)
        imp_t = jnp.where(blk_row <= cur_row, jnp.where(forced, SEL_FORCE_SCORE, imp_t), NEG_INF)
        rank = jnp.zeros((LANES, tq), F32)
        for i in range(n_sel):
            other = imp_t[i:i + 1, :]
            ahead = (other > imp_t) | ((other == imp_t) & (blk_row > i))
            rank = rank + jnp.where(ahead, 1.0, 0.0)
        sel = jnp.where(rank < top_k, 1.0, 0.0).T.astype(MXU_DTYPE)

        def sel_step(j, carry):
            m, l, acc = carry
            k0 = pl.multiple_of(j * NSA_TK, NSA_TK)
            s = _dot_nt(qr, ks_ref[0, pl.ds(k0, NSA_TK), gl]).reshape(heads_per_group, tq, NSA_TK)
            picked = jnp.dot(sel, exp_ref[j], preferred_element_type=F32)
            kpos = k0 + lax.broadcasted_iota(jnp.int32, (1, NSA_TK), 1)
            ok = (picked > 0.5) & (kpos <= tcol)
            s = jnp.where(ok[None], s, NEG_INF)
            m_new = jnp.maximum(m, jnp.max(s, axis=-1, keepdims=True))
            alpha = jnp.exp(m - m_new)
            e = jnp.exp(s - m_new)
            l = alpha * l + jnp.sum(e, axis=-1, keepdims=True)
            pv = _dot(e.reshape(rows, NSA_TK), vs_ref[0, pl.ds(k0, NSA_TK), gl])
            acc = alpha * acc + pv.reshape(heads_per_group, tq, LANES)
            return m_new, l, acc

        n_kt = (t0 + tq - 1) // NSA_TK + 1
        init = (jnp.full((heads_per_group, tq, 1), NEG_INF, F32), jnp.zeros((heads_per_group, tq, 1), F32),
                jnp.zeros((heads_per_group, tq, LANES), F32))
        _, l, acc = lax.fori_loop(0, n_kt, sel_step, init)
        unstack((acc / l).reshape(rows, LANES), g, y_sel)

        slab = NSA_WINDOW + tq
        w0 = pl.multiple_of(jnp.maximum(t0 - NSA_WINDOW, 0), tq)
        s = _dot_nt(qr, kw_ref[0, pl.ds(w0, slab), gl]).reshape(heads_per_group, tq, slab)
        kpos = w0 + lax.broadcasted_iota(jnp.int32, (1, slab), 1)
        ok = (kpos <= tcol) & (kpos > tcol - NSA_WINDOW)
        p = _softmax_last(jnp.where(ok[None], s, NEG_INF))
        unstack(_dot(p.reshape(rows, slab), vw_ref[0, pl.ds(w0, slab), gl]), g, y_win)

    gates = ng_ref[0]
    for c in range(4):
        sl = slice(c * LANES, (c + 1) * LANES)
        y_ref[0, :, sl] = (_dotx(gates, gexp_ref[0, :, sl]) * y_cmp[c] + _dotx(gates, gexp_ref[1, :, sl]) * y_sel[c]
                           + _dotx(gates, gexp_ref[2, :, sl]) * y_win[c])


def _nsa_constants(seq, n_cmp_pad):
    n_sel = seq // NSA_SEL_LEN
    cmp_start = np.arange(n_cmp_pad) * NSA_CMP_STRIDE
    sel_start = np.arange(LANES) * NSA_SEL_LEN
    ovl = ((cmp_start[:, None] < sel_start[None, :] + NSA_SEL_LEN) & (cmp_start[:, None] + NSA_CMP_LEN > sel_start[None, :])
           & (np.arange(LANES)[None, :] < n_sel) & (np.arange(n_cmp_pad)[:, None] < n_cmp_pad - 1))
    key_blk = np.arange(seq) // NSA_SEL_LEN
    expand = (np.arange(LANES)[:, None] == key_blk[None, :]).astype(np.float32)
    expand = expand.reshape(LANES, seq // NSA_TK, NSA_TK).transpose(1, 0, 2)
    gexp = np.zeros((3, LANES, NSA_HEADS * NSA_HEAD_DIM), np.float32)
    for i in range(3):
        for h in range(NSA_HEADS):
            gexp[i, 3 * h + i, h * NSA_HEAD_DIM:(h + 1) * NSA_HEAD_DIM] = 1.0
    return jnp.asarray(ovl.astype(np.float32)), jnp.asarray(expand, dtype=MXU_DTYPE), jnp.asarray(gexp)


def _nsa(z, kc2, vc2, batch, seq):
    tq = NSA_TQ
    q_w = NSA_HEADS * NSA_HEAD_DIM
    n_cmp_pad = kc2.shape[1]
    ovl, expand, gexp = _nsa_constants(seq, n_cmp_pad)
    r3 = lambda a: a.reshape(batch, seq, a.shape[-1])
    qblk = pl.BlockSpec((1, tq, q_w), lambda b, i: (b, i, 0))
    full = lambda n, w: pl.BlockSpec((1, n, w), lambda b, i: (b, 0, 0))
    top_k = min(NSA_SEL_TOPK, seq // NSA_SEL_LEN)
    return pl.pallas_call(
        functools.partial(_nsa_kernel, top_k=top_k),
        grid=(batch, seq // tq),
        in_specs=[qblk, qblk, full(n_cmp_pad, 256), full(n_cmp_pad, 256), full(seq, 256), full(seq, 256),
                  full(seq, 256), full(seq, 256), pl.BlockSpec((1, tq, LANES), lambda b, i: (b, i, 0)),
                  _const_spec(ovl.shape), _const_spec(expand.shape), _const_spec(gexp.shape)],
        out_specs=qblk,
        out_shape=jax.ShapeDtypeStruct((batch, seq, q_w), F32),
        compiler_params=_cparams("parallel", "arbitrary"),
    )(r3(z["qc"]), r3(z["qr"]), kc2, vc2, r3(z["ks"]), r3(z["vs"]), r3(z["kw"]), r3(z["vw"]), r3(z["ng"]),
      ovl, expand, gexp)


RWKV_CHUNK = 64
RWKV_TB = 256


def _rwkv_kernel(z_ref, mu_ref, vecs_ref, wa2_ref, g2_ref, ones_ref, o_ref, prev_ref, st_ref):
    tb = z_ref.shape[1]
    width = RWKV_HEADS * RWKV_HEAD_DIM
    n = RWKV_HEAD_DIM
    c = RWKV_CHUNK

    @pl.when(pl.program_id(1) == 0)
    def _():
        prev_ref[...] = jnp.zeros_like(prev_ref)
        st_ref[...] = jnp.zeros_like(st_ref)

    z = z_ref[0]
    row = lax.broadcasted_iota(jnp.int32, (tb, 1), 0)
    z_prev = jnp.where(row == 0, prev_ref[...], pltpu.roll(z, 1, 0))
    prev_ref[...] = z[tb - 1:tb, :]
    zs = z + (z_prev - z) * mu_ref[...]

    w0, a0, k_k, k_a, r_k, ln_w, ln_b = (vecs_ref[i:i + 1, :] for i in range(7))
    r = zs[:, 0:width]
    k = zs[:, width:2 * width]
    v = zs[:, 2 * width:3 * width]
    lr = zs[:, 3 * width:3 * width + LANES]
    w = w0 + _dot(jnp.tanh(lr), wa2_ref[0])
    w = -(jnp.maximum(-w, 0.0) + jnp.log(1.0 + jnp.exp(-jnp.abs(w)))) - 0.5
    logd = -jnp.exp(w)
    a = _sigmoid(a0 + _dot(lr, wa2_ref[1]))
    gate = _dot(_sigmoid(zs[:, 3 * width + LANES:]), g2_ref[...])
    ones_h = ones_ref[...]
    kk = k * k_k
    kk = kk / jnp.maximum(jnp.sqrt(_dotx(kk * kk, ones_h)), 1e-12)
    k = k * (1.0 + (a - 1.0) * k_a)
    bonus = _dotx(r * k * r_k, ones_h) * v

    ri = lax.broadcasted_iota(jnp.int32, (tb, tb), 0)
    ci = lax.broadcasted_iota(jnp.int32, (tb, tb), 1)
    same = (ri // c) == (ci // c)
    cl = _dotx(jnp.where(same & (ci <= ri), 1.0, 0.0), logd)
    cend = _dotx(jnp.where(same, 1.0, 0.0), logd)
    e_inv = jnp.exp(-cl)
    e_end = jnp.exp(cend - cl)
    a_t = -kk * jnp.exp(cl - logd)
    r_t = r * jnp.exp(cl)
    b = kk * a
    b_t = b * e_inv
    k_t = k * e_inv
    b_end_T = (b * e_end).T
    k_end_T = (k * e_end).T
    w_end = jnp.exp(cend)

    i64 = lax.broadcasted_iota(jnp.int32, (c, c), 0)
    j64 = lax.broadcasted_iota(jnp.int32, (c, c), 1)
    strict = j64 < i64
    incl = j64 <= i64
    eye = jnp.where(i64 == j64, 1.0, 0.0)

    for h in range(RWKV_HEADS):
        hs = slice(h * n, (h + 1) * n)
        maps = []
        for q in range(tb // c):
            qs = slice(q * c, (q + 1) * c)
            at, rt, bt, kt, vv = a_t[qs, hs], r_t[qs, hs], b_t[qs, hs], k_t[qs, hs], v[qs, hs]
            p = _dotx_nt(jnp.concatenate([at, rt], axis=0), jnp.concatenate([bt, kt], axis=0))
            n_ab = jnp.where(strict, p[:c, :c], 0.0)
            n_ak = jnp.where(strict, p[:c, c:], 0.0)
            d_rb = jnp.where(incl, p[c:, :c], 0.0)
            d_rk = jnp.where(incl, p[c:, c:], 0.0)
            t_inv = eye + n_ab
            n_pow = n_ab
            for _ in range(int(math.log2(c)) - 1):
                n_pow = _dotx(n_pow, n_pow)
                t_inv = t_inv + _dotx(n_pow, t_inv)
            mg1 = _dotx(t_inv, jnp.concatenate([at, _dotx(n_ak, vv)], axis=1))
            dm = _dotx(d_rb, mg1)
            m2 = rt + dm[:, :n]
            g2 = dm[:, n:] + _dotx(d_rk, vv)
            bm = _dotx(b_end_T[hs, qs], mg1)
            m3 = jnp.where(i64 == j64, w_end[q * c:q * c + 1, hs], 0.0) + bm[:, :n]
            g3 = bm[:, n:] + _dotx(k_end_T[hs, qs], vv)
            maps.append((jnp.concatenate([m3, m2], axis=0), jnp.concatenate([g3, g2], axis=0)))
        st = st_ref[h]
        ys = []
        for mm, gg in maps:
            both = _dotx(mm, st) + gg
            st = both[:n]
            ys.append(both[n:])
        st_ref[h] = st
        y = jnp.concatenate(ys, axis=0)
        mean = jnp.mean(y, axis=-1, keepdims=True)
        var = jnp.mean(jnp.square(y - mean), axis=-1, keepdims=True)
        o_ref[0, :, hs] = (y - mean) * lax.rsqrt(var + RWKV_LN_EPS)
    o_ref[0] = ((o_ref[0] * ln_w + ln_b) + bonus) * gate


def _rwkv(z_rw, mu, vecs, wa2, g2, batch, seq):
    tb = min(RWKV_TB, seq)
    width = RWKV_HEADS * RWKV_HEAD_DIM
    head = np.arange(width) // RWKV_HEAD_DIM
    ones_h = jnp.asarray((head[:, None] == head[None, :]).astype(np.float32))
    z3 = z_rw.reshape(batch, seq, z_rw.shape[-1])
    return pl.pallas_call(
        _rwkv_kernel,
        grid=(batch, seq // tb),
        in_specs=[pl.BlockSpec((1, tb, z3.shape[-1]), lambda b, i: (b, i, 0)), _const_spec(mu.shape),
                  _const_spec(vecs.shape), _const_spec(wa2.shape), _const_spec(g2.shape), _const_spec(ones_h.shape)],
        out_specs=pl.BlockSpec((1, tb, width), lambda b, i: (b, i, 0)),
        out_shape=jax.ShapeDtypeStruct((batch, seq, width), F32),
        scratch_shapes=[pltpu.VMEM((1, z3.shape[-1]), F32), pltpu.VMEM((RWKV_HEADS, RWKV_HEAD_DIM, RWKV_HEAD_DIM), F32)],
        compiler_params=_cparams("parallel", "arbitrary"),
    )(z3, mu, vecs, wa2, g2, ones_h)


S5_TB = 512


def _s5_kernel(u_ref, win_ref, step_ref, pow_ref, cout_ref, d_ref, wglu_ref, o_ref, xr_ref, xi_ref, carry_ref):
    tb = u_ref.shape[1]
    ns = S5_GROUPS * S5_STATE

    @pl.when(pl.program_id(1) == 0)
    def _():
        carry_ref[...] = jnp.zeros_like(carry_ref)

    u = u_ref[0]
    bu = _dot(u, win_ref[...])
    xr, xi = bu[:, :ns], bu[:, ns:]
    row = lax.broadcasted_iota(jnp.int32, (tb, 1), 0) % SUBLANES
    for lvl in range(3):
        s = 1 << lvl
        ar, ai = step_ref[2 * lvl:2 * lvl + 1, :], step_ref[2 * lvl + 1:2 * lvl + 2, :]
        pr, pi = pltpu.roll(xr, s, 0), pltpu.roll(xi, s, 0)
        take = row >= s
        xr, xi = (xr + jnp.where(take, ar * pr - ai * pi, 0.0), xi + jnp.where(take, ar * pi + ai * pr, 0.0))
    xr_ref[...] = xr
    xi_ref[...] = xi
    pw_r, pw_i = pow_ref[0], pow_ref[1]

    def tile(j, carry):
        cr, ci = carry
        rows = pl.ds(pl.multiple_of(j * SUBLANES, SUBLANES), SUBLANES)
        tr = xr_ref[rows, :] + (pw_r * cr - pw_i * ci)
        ti = xi_ref[rows, :] + (pw_r * ci + pw_i * cr)
        xr_ref[rows, :] = tr
        xi_ref[rows, :] = ti
        return tr[SUBLANES - 1:SUBLANES, :], ti[SUBLANES - 1:SUBLANES, :]

    cr, ci = lax.fori_loop(0, tb // SUBLANES, tile, (carry_ref[0:1, :], carry_ref[1:2, :]))
    carry_ref[0:1, :] = cr
    carry_ref[1:2, :] = ci
    y = _dot(xr_ref[...], cout_ref[0]) + _dot(xi_ref[...], cout_ref[1]) + d_ref[...] * u
    y = _gelu(y)
    o_ref[0] = y * _sigmoid(_dot(y, wglu_ref[...]))


def _s5_weights(lam_re, lam_im, log_dt, b_re, b_im, c_re, c_im, d, w_glu):
    g, p, hw = S5_GROUPS, S5_STATE, S5_GROUP_WIDTH
    dt = jnp.exp(log_dt.astype(F32))[:, None]
    lr, li = lam_re.astype(F32), lam_im.astype(F32)
    mag = jnp.exp(lr * dt)
    ar, ai = mag * jnp.cos(li * dt), mag * jnp.sin(li * dt)
    den = lr * lr + li * li
    cr = ((ar - 1.0) * lr + ai * li) / den
    ci = (ai * lr - (ar - 1.0) * li) / den
    eye = jnp.eye(g, dtype=F32)
    w_re = cr[:, :, None] * b_re - ci[:, :, None] * b_im
    w_im = cr[:, :, None] * b_im + ci[:, :, None] * b_re
    bd_in = lambda w: jnp.einsum("gph,ga->ghap", w, eye).reshape(g * hw, g * p)
    win = jnp.concatenate([bd_in(w_re), bd_in(w_im)], axis=1).astype(MXU_DTYPE)
    bd_out = lambda w: jnp.einsum("ghp,ga->gpah", w, eye).reshape(g * p, g * hw)
    cout = jnp.stack([bd_out(c_re.astype(F32)), -bd_out(c_im.astype(F32))]).astype(MXU_DTYPE)
    flat = lambda t: t.reshape(1, g * p)
    cmul = lambda x, y: (x[0] * y[0] - x[1] * y[1], x[0] * y[1] + x[1] * y[0])
    a1 = (flat(ar), flat(ai))
    a2 = cmul(a1, a1)
    a4 = cmul(a2, a2)
    step = jnp.concatenate([a1[0], a1[1], a2[0], a2[1], a4[0], a4[1], jnp.zeros((2, g * p), F32)], axis=0)
    pows = [a1]
    for _ in range(SUBLANES - 1):
        pows.append(cmul(pows[-1], a1))
    powers = jnp.stack([jnp.concatenate([q[0] for q in pows], axis=0), jnp.concatenate([q[1] for q in pows], axis=0)])
    return win, step, powers, cout, d.reshape(1, g * hw).astype(F32), w_glu.astype(MXU_DTYPE)


def _s5(u, weights, batch, seq):
    win, step, powers, cout, d, wglu = weights
    tb = min(S5_TB, seq)
    ns = S5_GROUPS * S5_STATE
    width = S5_GROUPS * S5_GROUP_WIDTH
    u3 = u.reshape(batch, seq, width)
    blk = pl.BlockSpec((1, tb, width), lambda b, i: (b, i, 0))
    return pl.pallas_call(
        _s5_kernel,
        grid=(batch, seq // tb),
        in_specs=[blk] + [_const_spec(a.shape) for a in (win, step, powers, cout, d, wglu)],
        out_specs=blk,
        out_shape=jax.ShapeDtypeStruct((batch, seq, width), F32),
        scratch_shapes=[pltpu.VMEM((tb, ns), F32), pltpu.VMEM((tb, ns), F32), pltpu.VMEM((SUBLANES, ns), F32)],
        compiler_params=_cparams("parallel", "arbitrary"),
    )(u3, win, step, powers, cout, d, wglu)


def _merge_kernel(x_ref, yn_ref, yr_ref, ys_ref, mg_ref, wun_ref, wur_ref, wus_ref, wout_ref, g_ref, o_ref):
    d = x_ref.shape[1]
    merged = (mg_ref[:, 0:d] * _dot(yn_ref[...], wun_ref[...]) + mg_ref[:, d:2 * d] * _dot(yr_ref[...], wur_ref[...])
              + mg_ref[:, 2 * d:3 * d] * _dot(ys_ref[...], wus_ref[...]))
    o_ref[...] = x_ref[...] + _rms(_dot(merged, wout_ref[...]), g_ref[...])


def _merge(x2, y_nsa, y_rwkv, y_s5, mg, wun, wur, wus, wout, gain, tm):
    m, d = x2.shape
    row = lambda a: pl.BlockSpec((tm, a.shape[1]), lambda i: (i, 0))
    return pl.pallas_call(
        _merge_kernel,
        grid=(m // tm,),
        in_specs=[row(x2), row(y_nsa), row(y_rwkv), row(y_s5), row(mg)]
        + [_const_spec(a.shape) for a in (wun, wur, wus, wout, gain)],
        out_specs=row(x2),
        out_shape=jax.ShapeDtypeStruct((m, d), F32),
        compiler_params=_cparams("parallel"),
    )(x2, y_nsa, y_rwkv, y_s5, mg, wun, wur, wus, wout, gain)


def _memkv_kernel(m_ref, g_ref, w_ref, o_ref):
    o_ref[0] = _dot(_rms(m_ref[0], g_ref[...]), w_ref[...]).astype(o_ref.dtype)


def _memkv(mem, gain, wkv):
    b, n, d = mem.shape
    return pl.pallas_call(
        _memkv_kernel,
        grid=(b,),
        in_specs=[pl.BlockSpec((1, n, d), lambda i: (i, 0, 0)), _const_spec(gain.shape), _const_spec(wkv.shape)],
        out_specs=pl.BlockSpec((1, n, wkv.shape[1]), lambda i: (i, 0, 0)),
        out_shape=jax.ShapeDtypeStruct((b, n, wkv.shape[1]), MXU_DTYPE),
        compiler_params=_cparams("parallel"),
    )(mem, gain, wkv)


def _xattn_kernel(x_ref, kv_ref, gin_ref, gout_ref, wq_ref, wo_ref, o_ref):
    x = x_ref[0]
    width = XA_HEADS * XA_HEAD_DIM
    q = (_dot(_rms(x, gin_ref[...]), wq_ref[...]) * (XA_HEAD_DIM ** -0.5)).astype(MXU_DTYPE)
    k = kv_ref[0, :, 0:width]
    v = kv_ref[0, :, width:2 * width]
    head = lax.broadcasted_iota(jnp.int32, q.shape, 1) // XA_HEAD_DIM
    att = jnp.zeros(q.shape, F32)
    for h in range(XA_HEADS):
        mine = head == h
        p = _softmax_last(_dot_nt(jnp.where(mine, q, jnp.zeros_like(q)), k))
        att = att + jnp.where(mine, _dot(p, v), 0.0)
    o_ref[0] = x + _rms(_dot(att, wo_ref[...]), gout_ref[...])


def _xattn(x3, kv, gin, gout, wq, wo, tm):
    b, s, d = x3.shape
    blk = pl.BlockSpec((1, tm, d), lambda i, j: (i, j, 0))
    return pl.pallas_call(
        _xattn_kernel,
        grid=(b, s // tm),
        in_specs=[blk, pl.BlockSpec((1,) + kv.shape[1:], lambda i, j: (i, 0, 0))]
        + [_const_spec(a.shape) for a in (gin, gout, wq, wo)],
        out_specs=blk,
        out_shape=jax.ShapeDtypeStruct(x3.shape, F32),
        compiler_params=_cparams("parallel", "parallel"),
    )(x3, kv, gin, gout, wq, wo)


FFN_CHUNK = 256


def _ffn_kernel(x_ref, gin_ref, gout_ref, wg_ref, wu_ref, wd_ref, o_ref):
    x = x_ref[...]
    h = _rms(x, gin_ref[...]).astype(MXU_DTYPE)
    acc = jnp.zeros(x.shape, F32)
    for c in range(wg_ref.shape[1] // FFN_CHUNK):
        sl = slice(c * FFN_CHUNK, (c + 1) * FFN_CHUNK)
        a = jnp.dot(h, wg_ref[:, sl], preferred_element_type=F32)
        b = jnp.dot(h, wu_ref[:, sl], preferred_element_type=F32)
        acc = acc + _dot(a * _sigmoid(a) * b, wd_ref[sl, :])
    o_ref[...] = x + _rms(acc, gout_ref[...])


def _ffn(x2, gin, gout, wg, wu, wd, tm):
    m, d = x2.shape
    assert wg.shape[1] % FFN_CHUNK == 0
    row = pl.BlockSpec((tm, d), lambda i: (i, 0))
    return pl.pallas_call(
        _ffn_kernel,
        grid=(m // tm,),
        in_specs=[row] + [_const_spec(a.shape) for a in (gin, gout, wg, wu, wd)],
        out_specs=row,
        out_shape=jax.ShapeDtypeStruct((m, d), F32),
        compiler_params=_cparams("parallel"),
    )(x2, gin, gout, wg, wu, wd)


ROW_TILE = 256


def kernel(x, mem, norm_gains, mem_norm, w_in, nsa_cmp_pos_k, nsa_cmp_pos_v, nsa_ck_w1, nsa_ck_w2, nsa_cv_w1, nsa_cv_w2, rwkv_mu, rwkv_w0, rwkv_w2, rwkv_a0, rwkv_a2, rwkv_g2, rwkv_k_k, rwkv_k_a, rwkv_r_k, rwkv_ln_w, rwkv_ln_b, s5_lam_re, s5_lam_im, s5_log_dt, s5_b_re, s5_b_im, s5_c_re, s5_c_im, s5_d, s5_w_glu, w_up_nsa, w_up_rwkv, w_up_s5, w_out, xa_w_q, xa_w_k, xa_w_v, xa_w_o, ffn_w_gate, ffn_w_up, ffn_w_down):
    batch, seq, d = x.shape
    depth = w_in.shape[0]
    tm = min(ROW_TILE, seq)
    bf = lambda a: a.astype(MXU_DTYPE)
    gain = lambda l, i: norm_gains[l, i].reshape(1, d)

    half = NSA_HEAD_DIM // 2
    inv = 1.0 / (ROPE_THETA ** (jnp.arange(0, NSA_HEAD_DIM, 2, dtype=F32) / NSA_HEAD_DIM))
    ang = jnp.arange(seq, dtype=F32)[:, None] * inv[None, :]
    cos, sin = jnp.cos(ang), jnp.sin(ang)
    cos128 = jnp.tile(cos, (1, LANES // half))
    sin128 = jnp.tile(jnp.concatenate([-sin, sin], axis=1), (1, LANES // NSA_HEAD_DIM))

    x2 = x.reshape(batch * seq, d)
    for l in range(depth):
        z = _inproj(x2, gain(l, 0), cos128, sin128, _regroup_w_in(w_in[l]), seq, tm)

        chunks = lambda a: a.reshape(batch, seq // NSA_CMP_STRIDE, NSA_CMP_STRIDE * a.shape[-1])
        kc2, vc2 = _compress(chunks(z["kc"]), chunks(z["vc"]),
                             *_compress_weights(nsa_cmp_pos_k[l], nsa_ck_w1[l], nsa_ck_w2[l]),
                             *_compress_weights(nsa_cmp_pos_v[l], nsa_cv_w1[l], nsa_cv_w2[l]))
        y_nsa = _nsa(z, kc2, vc2, batch, seq)

        rank = rwkv_w2.shape[1]
        width = RWKV_HEADS * RWKV_HEAD_DIM
        zero = jnp.zeros((rank, width), F32)
        wa2 = bf(jnp.stack([jnp.concatenate([rwkv_w2[l], zero], axis=0), jnp.concatenate([zero, rwkv_a2[l]], axis=0)]))
        vecs = jnp.stack([rwkv_w0[l], rwkv_a0[l], rwkv_k_k[l], rwkv_k_a[l], rwkv_r_k[l].reshape(width), rwkv_ln_w[l],
                          rwkv_ln_b[l], jnp.zeros((width,), F32)])
        y_rwkv = _rwkv(z["rw"], rwkv_mu[l].reshape(1, -1), vecs, wa2, bf(rwkv_g2[l]), batch, seq)

        y_s5 = _s5(z["s5"], _s5_weights(s5_lam_re[l], s5_lam_im[l], s5_log_dt[l], s5_b_re[l], s5_b_im[l], s5_c_re[l],
                                        s5_c_im[l], s5_d[l], s5_w_glu[l]), batch, seq)

        flat = lambda a: a.reshape(batch * seq, a.shape[-1])
        x2 = _merge(x2, flat(y_nsa), flat(y_rwkv), flat(y_s5), z["mg"], bf(w_up_nsa[l]), bf(w_up_rwkv[l]),
                    bf(w_up_s5[l]), bf(w_out[l]), gain(l, 1), tm)

        kv = _memkv(mem, mem_norm[l].reshape(1, d), bf(jnp.concatenate([xa_w_k[l], xa_w_v[l]], axis=1)))
        x2 = _xattn(x2.reshape(batch, seq, d), kv, gain(l, 2), gain(l, 3), bf(xa_w_q[l]), bf(xa_w_o[l]),
                    tm).reshape(batch * seq, d)

        x2 = _ffn(x2, gain(l, 4), gain(l, 5), bf(ffn_w_gate[l]), bf(ffn_w_up[l]), bf(ffn_w_down[l]), tm)
    return x2.reshape(batch, seq, d)
```

```python
import functools
import math

import jax
import jax.numpy as jnp
import numpy as np
from jax import lax
from jax.experimental import pallas as pl
from jax.experimental.pallas import tpu as pltpu

NSA_HEADS = 8
NSA_KV_GROUPS = 2
NSA_HEAD_DIM = 64
NSA_CMP_LEN = 32
NSA_CMP_STRIDE = 16
NSA_SEL_LEN = 64
NSA_SEL_TOPK = 16
NSA_WINDOW = 512
RWKV_HEADS = 4
RWKV_HEAD_DIM = 64
RWKV_LN_EPS = 64e-5
S5_GROUPS = 16
S5_GROUP_WIDTH = 16
S5_STATE = 64
XA_HEADS = 4
XA_HEAD_DIM = 64
ROPE_THETA = 10000.0
NORM_EPS = 1e-6
NEG_INF = -1e30
SEL_FORCE_SCORE = 1e9

LANES = 128
SUBLANES = 8
MXU_DTYPE = jnp.bfloat16
VMEM_LIMIT = 56 << 20

F32 = jnp.float32


def _cparams(*sem):
    return pltpu.CompilerParams(dimension_semantics=sem, vmem_limit_bytes=VMEM_LIMIT)


def _const_spec(shape):
    nd = len(shape)
    return pl.BlockSpec(shape, lambda *_: (0,) * nd, pipeline_mode=pl.Buffered(1))


def _dot(a, b):
    return jnp.dot(a.astype(MXU_DTYPE), b.astype(MXU_DTYPE), preferred_element_type=F32)


def _dot_nt(a, b):
    return lax.dot_general(a.astype(MXU_DTYPE), b.astype(MXU_DTYPE), (((1,), (1,)), ((), ())),
                           preferred_element_type=F32)


def _split(a, terms):
    parts = []
    for _ in range(terms - 1):
        p = a.astype(MXU_DTYPE)
        parts.append(p)
        a = a - p.astype(F32)
    parts.append(a.astype(MXU_DTYPE))
    return parts


def _dot_split_lhs(a, b01, terms=3):
    b01 = b01.astype(MXU_DTYPE)
    return sum(jnp.dot(p, b01, preferred_element_type=F32) for p in _split(a, terms))


def _dot_split_rhs(a01, b, terms=3):
    a01 = a01.astype(MXU_DTYPE)
    return sum(jnp.dot(a01, p, preferred_element_type=F32) for p in _split(b, terms))


def _rms(x, g):
    return x * lax.rsqrt(jnp.mean(x * x, axis=-1, keepdims=True) + NORM_EPS) * g


def _sigmoid(x):
    return 1.0 / (1.0 + jnp.exp(-x))


def _gelu(x):
    return 0.5 * x * (1.0 + jnp.tanh(math.sqrt(2.0 / math.pi) * (x + 0.044715 * (x * x * x))))


def _softmax_last(s):
    m = jnp.max(s, axis=-1, keepdims=True)
    e = jnp.exp(s - m)
    return e / jnp.sum(e, axis=-1, keepdims=True)


_SEG = dict(q=(0, 512), kc=(512, 640), vc=(640, 768), ks=(768, 1024), vs=(1024, 1280), kw=(1280, 1536),
            vw=(1536, 1792), ng=(1792, 1920), rw=(1920, 2944), s5=(2944, 3200), mg=(3200, 6272))
_W_IN_COLS = 6272


def _rope128(x, cos, sin_signed):
    lane = lax.broadcasted_iota(jnp.int32, x.shape, 1)
    first = (lane % NSA_HEAD_DIM) < (NSA_HEAD_DIM // 2)
    rot = jnp.where(first, pltpu.roll(x, LANES - NSA_HEAD_DIM // 2, 1), pltpu.roll(x, NSA_HEAD_DIM // 2, 1))
    return x * cos + rot * sin_signed


def _inproj_kernel(x_ref, g_ref, cos_ref, sin_ref, w_ref,
                   qc_o, qr_o, kc_o, vc_o, ks_o, vs_o, kw_o, vw_o, ng_o, rw_o, s5_o, mg_o, *, n_tab):
    h = _rms(x_ref[...], g_ref[...]).astype(MXU_DTYPE)
    cos = cos_ref[...]
    sin = sin_ref[...]
    scale = NSA_HEAD_DIM ** -0.5

    def seg(name):
        a, b = _SEG[name]
        return jnp.dot(h, w_ref[:, a:b], preferred_element_type=F32)

    q = seg("q")
    qc_o[...] = (q * scale).astype(qc_o.dtype)
    for c in range(q.shape[1] // LANES):
        sl = slice(c * LANES, (c + 1) * LANES)
        qr_o[:, sl] = (_rope128(q[:, sl], cos, sin) * scale).astype(qr_o.dtype)
    kc_o[...] = seg("kc")
    vc_o[...] = seg("vc")
    k = seg("kw")
    for c in range(NSA_KV_GROUPS):
        sl = slice(c * LANES, (c + 1) * LANES)
        kw_o[:, sl] = _rope128(k[:, sl], cos, sin).astype(kw_o.dtype)
    tm = x_ref.shape[0]
    tok = (pl.program_id(0) % n_tab) * tm + lax.broadcasted_iota(jnp.int32, (tm, LANES), 0)
    onehot = jnp.where(lax.broadcasted_iota(jnp.int32, (tm, LANES), 1) == tok // NSA_SEL_LEN, 1.0, 0.0)
    k = seg("ks")
    for c in range(NSA_KV_GROUPS):
        ks_o[:, 2 * c * LANES:(2 * c + 1) * LANES] = _rope128(k[:, c * LANES:(c + 1) * LANES], cos, sin).astype(ks_o.dtype)
        ks_o[:, (2 * c + 1) * LANES:(2 * c + 2) * LANES] = onehot.astype(ks_o.dtype)
    vs_o[...] = seg("vs").astype(vs_o.dtype)
    vw_o[...] = seg("vw").astype(vw_o.dtype)
    ng_o[...] = _sigmoid(seg("ng"))
    rw_o[...] = seg("rw")
    s5_o[...] = seg("s5")
    mg_o[...] = _sigmoid(seg("mg"))


def _dup_groups(w):
    hd = NSA_HEAD_DIM
    return jnp.concatenate([w[:, :hd], w[:, :hd], w[:, hd:], w[:, hd:]], axis=1)


def _regroup_w_in(w):
    d = w.shape[0]
    q_w = NSA_HEADS * NSA_HEAD_DIM
    kv_w = NSA_KV_GROUPS * NSA_HEAD_DIM
    o = 0
    wq = w[:, o:o + q_w]; o += q_w
    kv = [w[:, o + i * kv_w:o + (i + 1) * kv_w] for i in range(6)]; o += 6 * kv_w
    wng = w[:, o:o + 3 * NSA_HEADS]; o += 3 * NSA_HEADS
    rw_w = 3 * RWKV_HEADS * RWKV_HEAD_DIM + 256
    wrw = w[:, o:o + rw_w]; o += rw_w
    s5_w = S5_GROUPS * S5_GROUP_WIDTH
    ws5 = w[:, o:o + s5_w]; o += s5_w
    wmg = w[:, o:]
    wng = jnp.pad(wng, ((0, 0), (0, LANES - wng.shape[1])))
    out = jnp.concatenate([wq, kv[0], kv[1], _dup_groups(kv[2]), _dup_groups(kv[3]), _dup_groups(kv[4]),
                           _dup_groups(kv[5]), wng, wrw, ws5, wmg], axis=1)
    assert out.shape == (d, _W_IN_COLS), out.shape
    return out.astype(MXU_DTYPE)


def _inproj(x2, gain, cos128, sin128, w_all, seq, tm):
    m, d = x2.shape
    n_tab = seq // tm
    row = lambda w: pl.BlockSpec((tm, w), lambda i: (i, 0))
    tab = pl.BlockSpec((tm, LANES), lambda i: (i % n_tab, 0))
    widths = dict(qc=512, qr=512, kc=128, vc=128, ks=512, vs=256, kw=256, vw=256, ng=128, rw=1024, s5=256, mg=3072)
    dtypes = dict(qc=MXU_DTYPE, qr=MXU_DTYPE, kc=F32, vc=F32, ks=MXU_DTYPE, vs=MXU_DTYPE, kw=MXU_DTYPE,
                  vw=MXU_DTYPE, ng=F32, rw=F32, s5=F32, mg=F32)
    names = list(widths)
    outs = pl.pallas_call(
        functools.partial(_inproj_kernel, n_tab=n_tab),
        grid=(m // tm,),
        in_specs=[row(d), _const_spec((1, d)), tab, tab, _const_spec(w_all.shape)],
        out_specs=[row(widths[n]) for n in names],
        out_shape=[jax.ShapeDtypeStruct((m, widths[n]), dtypes[n]) for n in names],
        compiler_params=_cparams("parallel"),
    )(x2, gain, cos128, sin128, w_all)
    return dict(zip(names, outs))


def _compress_kernel(k_ref, v_ref, pk_ref, pv_ref, kw1_ref, kw2_ref, vw1_ref, vw2_ref, kc_o, vc_o):
    def one(x_ref, p_ref, w1_ref, w2_ref, out):
        x = x_ref[0]
        n = x.shape[0]
        top = _dot(x + p_ref[0:1, :], w1_ref[0])
        bot = _dot(x + p_ref[1:2, :], w1_ref[1])
        hid = top + pltpu.roll(bot, n - 1, 0)
        out[0] = _dot(_gelu(hid), w2_ref[...]).astype(out.dtype)

    one(k_ref, pk_ref, kw1_ref, kw2_ref, kc_o)
    one(v_ref, pv_ref, vw1_ref, vw2_ref, vc_o)


def _compress_weights(pos, w1, w2):
    g, hd, half = NSA_KV_GROUPS, NSA_HEAD_DIM, NSA_CMP_STRIDE
    hidden = w1.shape[1]
    eye = jnp.eye(g, dtype=w1.dtype)
    w1r = w1.reshape(2, half, hd, hidden)
    w1e = jnp.einsum("pldj,ab->pladbj", w1r, eye).reshape(2, half * g * hd, g * hidden)
    w2d = jnp.concatenate([w2, w2], axis=1)
    w2e = jnp.einsum("jd,ab->ajbd", w2d, eye).reshape(g * hidden, g * 2 * hd)
    pos_e = jnp.broadcast_to(pos.reshape(2, half, 1, hd), (2, half, g, hd)).reshape(2, half * g * hd)
    return pos_e.astype(F32), w1e.astype(MXU_DTYPE), w2e.astype(MXU_DTYPE)


def _compress(kc, vc, pk, kw1, kw2, pv, vw1, vw2):
    b, n, width = kc.shape
    blk = pl.BlockSpec((1, n, width), lambda i: (i, 0, 0))
    out_w = kw2.shape[1]
    out_blk = pl.BlockSpec((1, n, out_w), lambda i: (i, 0, 0))
    return pl.pallas_call(
        _compress_kernel,
        grid=(b,),
        in_specs=[blk, blk, _const_spec(pk.shape), _const_spec(pv.shape), _const_spec(kw1.shape),
                  _const_spec(kw2.shape), _const_spec(vw1.shape), _const_spec(vw2.shape)],
        out_specs=[out_blk, out_blk],
        out_shape=[jax.ShapeDtypeStruct((b, n, out_w), MXU_DTYPE)] * 2,
        compiler_params=_cparams("parallel"),
    )(kc, vc, pk, pv, kw1, kw2, vw1, vw2)


NSA_TQ = 128
NSA_TK = 1024
MASKED = -1e30


def _nsa_kernel(qc_ref, qr_ref, kc_ref, vc_ref, ks_ref, vs_ref, kw_ref, vw_ref, ng_ref, ovl_ref, gexp_ref, y_ref, *, top_k):
    tq = qc_ref.shape[1]
    seq = ks_ref.shape[1]
    n_cmp_pad = kc_ref.shape[1]
    hpg = NSA_HEADS // NSA_KV_GROUPS
    rows = hpg * tq
    n_sel = seq // NSA_SEL_LEN
    n_sel_pad = -(-n_sel // SUBLANES) * SUBLANES
    t0 = pl.program_id(1) * tq
    tcol = t0 + lax.broadcasted_iota(jnp.int32, (tq, 1), 0)
    low_half = lax.broadcasted_iota(jnp.int32, (tq, LANES), 1) < NSA_HEAD_DIM

    def stack_heads(q_ref, g):
        parts = []
        for r in range(hpg):
            h = g * hpg + r
            slab = q_ref[0, :, (h // 2) * LANES:(h // 2 + 1) * LANES]
            keep = low_half if h % 2 == 0 else jnp.logical_not(low_half)
            parts.append(jnp.where(keep, slab, jnp.zeros_like(slab)))
        return jnp.concatenate(parts, axis=0)

    def unstack(o, g, y_parts):
        for pair in range(hpg // 2):
            even = o[(2 * pair) * tq:(2 * pair + 1) * tq]
            odd = o[(2 * pair + 1) * tq:(2 * pair + 2) * tq]
            y_parts[g * (hpg // 2) + pair] = jnp.where(low_half, even, odd)

    def masked_softmax_pv(s, bias, v, scale_rows=None):
        n = s.shape[-1]
        s = s.reshape(hpg, tq, n) + bias[None]
        e = jnp.exp(s - jnp.max(s, axis=-1, keepdims=True))
        inv = 1.0 / jnp.sum(e, axis=-1, keepdims=True)
        if scale_rows is not None:
            inv = inv * scale_rows[None]
        return e, inv, _dot(e.reshape(rows, n), v).reshape(hpg, tq, LANES) * inv

    cmp_end = lax.broadcasted_iota(jnp.int32, (1, n_cmp_pad), 1) * NSA_CMP_STRIDE + (NSA_CMP_LEN - 1)
    cmp_bias = jnp.where(cmp_end <= tcol, 0.0, NEG_INF)
    cmp_live = jnp.where(tcol >= NSA_CMP_LEN - 1, 1.0, 0.0)
    slab = NSA_WINDOW + tq
    w0 = pl.multiple_of(jnp.maximum(t0 - NSA_WINDOW, 0), tq)
    wpos = w0 + lax.broadcasted_iota(jnp.int32, (1, slab), 1)
    win_bias = jnp.where(wpos <= tcol, jnp.where(wpos > tcol - NSA_WINDOW, 0.0, NEG_INF), NEG_INF)
    n_kt = (t0 + tq - 1) // NSA_TK + 1
    k_last = pl.multiple_of((n_kt - 1) * NSA_TK, NSA_TK)
    diag_bias = jnp.where(k_last + lax.broadcasted_iota(jnp.int32, (1, NSA_TK), 1) <= tcol, 0.0, NEG_INF)

    blk = lax.broadcasted_iota(jnp.int32, (n_sel_pad, tq), 0)
    cur = (t0 + lax.broadcasted_iota(jnp.int32, (n_sel_pad, tq), 1)) // NSA_SEL_LEN
    forced = (blk == 0) | (blk == cur) | (blk == cur - 1)

    groups = range(NSA_KV_GROUPS)
    gl = [slice(g * LANES, (g + 1) * LANES) for g in groups]
    gx = [slice(2 * g * LANES, (2 * g + 2) * LANES) for g in groups]
    y_cmp, y_sel, y_win = [None] * 4, [None] * 4, [None] * 4
    qc = [stack_heads(qc_ref, g) for g in groups]
    qr = [stack_heads(qr_ref, g) for g in groups]

    cmp = [masked_softmax_pv(_dot_nt(qc[g], kc_ref[0, :, gl[g]]), cmp_bias, vc_ref[0, :, gl[g]], cmp_live) for g in groups]
    for g in groups:
        unstack(cmp[g][2].reshape(rows, LANES), g, y_cmp)

    qx = []
    for g in groups:
        e, inv, _ = cmp[g]
        imp = _dot_split_lhs(jnp.sum(e * inv, axis=0), ovl_ref[...])
        vals = jnp.where(blk <= cur, jnp.where(forced, SEL_FORCE_SCORE, imp.T[:n_sel_pad]), NEG_INF)
        rank = [jnp.zeros((SUBLANES, tq), F32) for _ in range(n_sel_pad // SUBLANES)]
        for i in range(n_sel):
            other = vals[i:i + 1, :]
            for u in range(n_sel_pad // SUBLANES):
                mine = vals[u * SUBLANES:(u + 1) * SUBLANES]
                if i < u * SUBLANES:
                    ahead = other >= mine
                elif i >= (u + 1) * SUBLANES:
                    ahead = other > mine
                else:
                    ahead = (other > mine) | ((other == mine) & (blk[u * SUBLANES:(u + 1) * SUBLANES] > i))
                rank[u] = rank[u] + jnp.where(ahead, 1.0, 0.0)
        drop = jnp.where(jnp.concatenate(rank, axis=0) < top_k, 0.0, MASKED)
        drop = jnp.concatenate([drop, jnp.zeros((LANES - n_sel_pad, tq), F32)], axis=0).T.astype(MXU_DTYPE)
        qx.append(jnp.concatenate([qr[g], jnp.concatenate([drop] * hpg, axis=0)], axis=1))

    def sel_tile(g, k0, bias, carry):
        m, l, acc = carry
        s = _dot_nt(qx[g], ks_ref[0, pl.ds(k0, NSA_TK), gx[g]]).reshape(hpg, tq, NSA_TK)
        if bias is not None:
            s = s + bias[None]
        m_new = jnp.maximum(m, jnp.max(s, axis=-1, keepdims=True))
        alpha = jnp.exp(m - m_new)
        e = jnp.exp(s - m_new)
        l = alpha * l + jnp.sum(e, axis=-1, keepdims=True)
        pv = _dot(e.reshape(rows, NSA_TK), vs_ref[0, pl.ds(k0, NSA_TK), gl[g]])
        return m_new, l, alpha * acc + pv.reshape(hpg, tq, LANES)

    def sel_tiles(k0, bias, carries):
        return tuple(sel_tile(g, k0, bias, carries[g]) for g in groups)

    init = (jnp.full((hpg, tq, 1), NEG_INF, F32), jnp.zeros((hpg, tq, 1), F32), jnp.zeros((hpg, tq, LANES), F32))
    carries = lax.fori_loop(0, n_kt - 1, lambda j, c: sel_tiles(pl.multiple_of(j * NSA_TK, NSA_TK), None, c),
                            tuple(init for _ in groups))
    carries = sel_tiles(k_last, diag_bias, carries)
    for g in groups:
        _, l, acc = carries[g]
        unstack((acc / l).reshape(rows, LANES), g, y_sel)

    win = [masked_softmax_pv(_dot_nt(qr[g], kw_ref[0, pl.ds(w0, slab), gl[g]]), win_bias, vw_ref[0, pl.ds(w0, slab), gl[g]])
           for g in groups]
    for g in groups:
        unstack(win[g][2].reshape(rows, LANES), g, y_win)

    gates = ng_ref[0]
    q_w = NSA_HEADS * NSA_HEAD_DIM
    spread = _dot_split_lhs(gates, gexp_ref[...], terms=2)
    for c in range(4):
        sl = slice(c * LANES, (c + 1) * LANES)
        y_ref[0, :, sl] = (spread[:, sl] * y_cmp[c] + spread[:, q_w + c * LANES:q_w + (c + 1) * LANES] * y_sel[c]
                           + spread[:, 2 * q_w + c * LANES:2 * q_w + (c + 1) * LANES] * y_win[c])


def _nsa_constants(seq, n_cmp_pad):
    n_sel = seq // NSA_SEL_LEN
    cmp_start = np.arange(n_cmp_pad) * NSA_CMP_STRIDE
    sel_start = np.arange(LANES) * NSA_SEL_LEN
    ovl = ((cmp_start[:, None] < sel_start[None, :] + NSA_SEL_LEN) & (cmp_start[:, None] + NSA_CMP_LEN > sel_start[None, :])
           & (np.arange(LANES)[None, :] < n_sel) & (np.arange(n_cmp_pad)[:, None] < n_cmp_pad - 1))
    q_w = NSA_HEADS * NSA_HEAD_DIM
    gexp = np.zeros((LANES, 3 * q_w), np.float32)
    for i in range(3):
        for h in range(NSA_HEADS):
            gexp[3 * h + i, i * q_w + h * NSA_HEAD_DIM:i * q_w + (h + 1) * NSA_HEAD_DIM] = 1.0
    return jnp.asarray(ovl.astype(np.float32), dtype=MXU_DTYPE), jnp.asarray(gexp, dtype=MXU_DTYPE)


def _nsa(z, kc2, vc2, batch, seq):
    tq = NSA_TQ
    q_w = NSA_HEADS * NSA_HEAD_DIM
    n_cmp_pad = kc2.shape[1]
    ovl, gexp = _nsa_constants(seq, n_cmp_pad)
    r3 = lambda a: a.reshape(batch, seq, a.shape[-1])
    qblk = pl.BlockSpec((1, tq, q_w), lambda b, i: (b, i, 0))
    full = lambda n, w: pl.BlockSpec((1, n, w), lambda b, i: (b, 0, 0))
    top_k = min(NSA_SEL_TOPK, seq // NSA_SEL_LEN)
    return pl.pallas_call(
        functools.partial(_nsa_kernel, top_k=top_k),
        grid=(batch, seq // tq),
        in_specs=[qblk, qblk, full(n_cmp_pad, 256), full(n_cmp_pad, 256), full(seq, 512), full(seq, 256),
                  full(seq, 256), full(seq, 256), pl.BlockSpec((1, tq, LANES), lambda b, i: (b, i, 0)),
                  _const_spec(ovl.shape), _const_spec(gexp.shape)],
        out_specs=qblk,
        out_shape=jax.ShapeDtypeStruct((batch, seq, q_w), F32),
        compiler_params=_cparams("parallel", "arbitrary"),
    )(r3(z["qc"]), r3(z["qr"]), kc2, vc2, r3(z["ks"]), r3(z["vs"]), r3(z["kw"]), r3(z["vw"]), r3(z["ng"]),
      ovl, gexp)


RWKV_CHUNK = 64
RWKV_TB = 256


def _rwkv_kernel(z_ref, mu_ref, vecs_ref, wa2_ref, g2_ref, ones_ref, o_ref, prev_ref, st_ref):
    tb = z_ref.shape[1]
    width = RWKV_HEADS * RWKV_HEAD_DIM
    n = RWKV_HEAD_DIM
    c = RWKV_CHUNK

    @pl.when(pl.program_id(1) == 0)
    def _():
        prev_ref[...] = jnp.zeros_like(prev_ref)
        st_ref[...] = jnp.zeros_like(st_ref)

    z = z_ref[0]
    row = lax.broadcasted_iota(jnp.int32, (tb, 1), 0)
    z_prev = jnp.where(row == 0, prev_ref[...], pltpu.roll(z, 1, 0))
    prev_ref[...] = z[tb - 1:tb, :]
    zs = z + (z_prev - z) * mu_ref[...]

    w0, a0, k_k, k_a, r_k, ln_w, ln_b = (vecs_ref[i:i + 1, :] for i in range(7))
    r = zs[:, 0:width]
    k = zs[:, width:2 * width]
    v = zs[:, 2 * width:3 * width]
    lr = zs[:, 3 * width:3 * width + LANES]
    w = w0 + _dot(jnp.tanh(lr), wa2_ref[0])
    w = -(jnp.maximum(-w, 0.0) + jnp.log(1.0 + jnp.exp(-jnp.abs(w)))) - 0.5
    logd = -jnp.exp(w)
    a = _sigmoid(a0 + _dot(lr, wa2_ref[1]))
    gate = _dot(_sigmoid(zs[:, 3 * width + LANES:]), g2_ref[...])
    ones_h = ones_ref[...]
    kk = k * k_k
    kk = kk / jnp.maximum(jnp.sqrt(_dot_split_lhs(kk * kk, ones_h)), 1e-12)
    k = k * (1.0 + (a - 1.0) * k_a)
    bonus = _dot_split_lhs(r * k * r_k, ones_h) * v

    ri = lax.broadcasted_iota(jnp.int32, (tb, tb), 0)
    ci = lax.broadcasted_iota(jnp.int32, (tb, tb), 1)
    same = (ri // c) == (ci // c)
    cl = _dot_split_rhs(jnp.where(same & (ci <= ri), 1.0, 0.0), logd)
    cend = _dot_split_rhs(jnp.where(same, 1.0, 0.0), logd)
    e_inv = jnp.exp(-cl)
    e_end = jnp.exp(cend - cl)
    a_t = -kk * jnp.exp(cl - logd)
    r_t = r * jnp.exp(cl)
    b = kk * a
    b_t = b * e_inv
    k_t = k * e_inv
    b_end_T = (b * e_end).T
    k_end_T = (k * e_end).T
    w_end = jnp.exp(cend)

    head_of_row = lax.broadcasted_iota(jnp.int32, (width, width), 0) // n
    head_of_lane = lax.broadcasted_iota(jnp.int32, (width, width), 1) // n
    on_diag_block = head_of_row == head_of_lane
    eye_w = lax.broadcasted_iota(jnp.int32, (width, width), 0) == lax.broadcasted_iota(jnp.int32, (width, width), 1)
    step = lax.broadcasted_iota(jnp.int32, (c, width), 0)
    within = lax.broadcasted_iota(jnp.int32, (c, width), 1) % n
    strict = within < step
    incl = within <= step
    eye_c = jnp.where(within == step, 1.0, 0.0)

    def bd(m):
        m = m.astype(MXU_DTYPE)
        return jnp.where(on_diag_block, jnp.concatenate([m] * RWKV_HEADS, axis=0), jnp.zeros((), MXU_DTYPE))

    chunk_ids = range(tb // c)
    qs = [slice(q * c, (q + 1) * c) for q in chunk_ids]
    xs = [jnp.concatenate([a_t[s], r_t[s]], axis=0) for s in qs]
    pb = [_dot_nt(xs[q], bd(b_t[qs[q]])) for q in chunk_ids]
    pk = [_dot_nt(xs[q], bd(k_t[qs[q]])) for q in chunk_ids]
    n_low = [jnp.where(strict, p[:c], 0.0) for p in pb]
    t_inv = [eye_c + jnp.where((step // 2) == (within // 2), m, 0.0) for m in n_low]
    size = 2
    while size < c:
        off = ((step // (2 * size)) == (within // (2 * size))) & ((step // size) != (within // size))
        y = [_dot(jnp.where(off, m, 0.0), bd(t)) for m, t in zip(n_low, t_inv)]
        t_inv = [t + _dot(t, bd(yy)) for t, yy in zip(t_inv, y)]
        size *= 2
    v_bd = [bd(v[s]) for s in qs]
    nv = [_dot(jnp.where(strict, pk[q][:c], 0.0), v_bd[q]) for q in chunk_ids]
    mg1 = [_dot(t_inv[q], jnp.concatenate([bd(a_t[qs[q]]), bd(nv[q])], axis=1)) for q in chunk_ids]
    dm = [_dot(jnp.where(incl, pb[q][c:], 0.0), jnp.concatenate([bd(mg1[q][:, :width]), bd(mg1[q][:, width:])], axis=1))
          for q in chunk_ids]
    drk_v = [_dot(jnp.where(incl, pk[q][c:], 0.0), v_bd[q]) for q in chunk_ids]
    bm = [_dot(b_end_T[:, qs[q]], mg1[q]) for q in chunk_ids]
    kv_end = [_dot(k_end_T[:, qs[q]], v[qs[q]]) for q in chunk_ids]
    maps = []
    for q in chunk_ids:
        m2 = r_t[qs[q]] + dm[q][:, :width]
        g2 = dm[q][:, width:] + drk_v[q]
        m3 = jnp.where(on_diag_block, bm[q][:, :width], 0.0) + jnp.where(eye_w, w_end[q * c:q * c + 1, :], 0.0)
        g3 = jnp.where(on_diag_block, bm[q][:, width:] + kv_end[q], 0.0)
        maps.append((m2, g2, m3, g3))

    st = st_ref[...]
    ys = []
    for m2, g2, m3, g3 in maps:
        ys.append(_dot(m2, st) + g2)
        st = _dot(m3, st) + g3
    st_ref[...] = st
    y = jnp.concatenate(ys, axis=0)
    inv_n = 1.0 / n
    mean = _dot_split_lhs(y, ones_h) * inv_n
    var = _dot_split_lhs(jnp.square(y - mean), ones_h) * inv_n
    yn = (y - mean) * lax.rsqrt(var + RWKV_LN_EPS)
    o_ref[0] = ((yn * ln_w + ln_b) + bonus) * gate


def _rwkv(z_rw, mu, vecs, wa2, g2, batch, seq):
    tb = min(RWKV_TB, seq)
    width = RWKV_HEADS * RWKV_HEAD_DIM
    head = np.arange(width) // RWKV_HEAD_DIM
    ones_h = jnp.asarray((head[:, None] == head[None, :]).astype(np.float32))
    z3 = z_rw.reshape(batch, seq, z_rw.shape[-1])
    return pl.pallas_call(
        _rwkv_kernel,
        grid=(batch, seq // tb),
        in_specs=[pl.BlockSpec((1, tb, z3.shape[-1]), lambda b, i: (b, i, 0)), _const_spec(mu.shape),
                  _const_spec(vecs.shape), _const_spec(wa2.shape), _const_spec(g2.shape), _const_spec(ones_h.shape)],
        out_specs=pl.BlockSpec((1, tb, width), lambda b, i: (b, i, 0)),
        out_shape=jax.ShapeDtypeStruct((batch, seq, width), F32),
        scratch_shapes=[pltpu.VMEM((1, z3.shape[-1]), F32), pltpu.VMEM((width, width), F32)],
        compiler_params=_cparams("parallel", "arbitrary"),
    )(z3, mu, vecs, wa2, g2, ones_h)


S5_TB = 512


def _s5_kernel(u_ref, win_ref, step_ref, pow_ref, cout_ref, d_ref, wglu_ref, o_ref, xr_ref, xi_ref, carry_ref):
    tb = u_ref.shape[1]
    ns = S5_GROUPS * S5_STATE

    @pl.when(pl.program_id(1) == 0)
    def _():
        carry_ref[...] = jnp.zeros_like(carry_ref)

    u = u_ref[0]
    bu = _dot(u, win_ref[...])
    xr, xi = bu[:, :ns], bu[:, ns:]
    row = lax.broadcasted_iota(jnp.int32, (tb, 1), 0) % SUBLANES
    for lvl in range(3):
        s = 1 << lvl
        ar, ai = step_ref[2 * lvl:2 * lvl + 1, :], step_ref[2 * lvl + 1:2 * lvl + 2, :]
        pr, pi = pltpu.roll(xr, s, 0), pltpu.roll(xi, s, 0)
        take = row >= s
        xr, xi = (xr + jnp.where(take, ar * pr - ai * pi, 0.0), xi + jnp.where(take, ar * pi + ai * pr, 0.0))
    xr_ref[...] = xr
    xi_ref[...] = xi
    pw_r, pw_i = pow_ref[0], pow_ref[1]

    def tile(j, carry):
        cr, ci = carry
        rows = pl.ds(pl.multiple_of(j * SUBLANES, SUBLANES), SUBLANES)
        tr = xr_ref[rows, :] + (pw_r * cr - pw_i * ci)
        ti = xi_ref[rows, :] + (pw_r * ci + pw_i * cr)
        xr_ref[rows, :] = tr
        xi_ref[rows, :] = ti
        return tr[SUBLANES - 1:SUBLANES, :], ti[SUBLANES - 1:SUBLANES, :]

    cr, ci = lax.fori_loop(0, tb // SUBLANES, tile, (carry_ref[0:1, :], carry_ref[1:2, :]))
    carry_ref[0:1, :] = cr
    carry_ref[1:2, :] = ci
    y = _dot(xr_ref[...], cout_ref[0]) + _dot(xi_ref[...], cout_ref[1]) + d_ref[...] * u
    y = _gelu(y)
    o_ref[0] = y * _sigmoid(_dot(y, wglu_ref[...]))


def _s5_weights(lam_re, lam_im, log_dt, b_re, b_im, c_re, c_im, d, w_glu):
    g, p, hw = S5_GROUPS, S5_STATE, S5_GROUP_WIDTH
    dt = jnp.exp(log_dt.astype(F32))[:, None]
    lr, li = lam_re.astype(F32), lam_im.astype(F32)
    mag = jnp.exp(lr * dt)
    ar, ai = mag * jnp.cos(li * dt), mag * jnp.sin(li * dt)
    den = lr * lr + li * li
    cr = ((ar - 1.0) * lr + ai * li) / den
    ci = (ai * lr - (ar - 1.0) * li) / den
    eye = jnp.eye(g, dtype=F32)
    w_re = cr[:, :, None] * b_re - ci[:, :, None] * b_im
    w_im = cr[:, :, None] * b_im + ci[:, :, None] * b_re
    bd_in = lambda w: jnp.einsum("gph,ga->ghap", w, eye).reshape(g * hw, g * p)
    win = jnp.concatenate([bd_in(w_re), bd_in(w_im)], axis=1).astype(MXU_DTYPE)
    bd_out = lambda w: jnp.einsum("ghp,ga->gpah", w, eye).reshape(g * p, g * hw)
    cout = jnp.stack([bd_out(c_re.astype(F32)), -bd_out(c_im.astype(F32))]).astype(MXU_DTYPE)
    flat = lambda t: t.reshape(1, g * p)
    cmul = lambda x, y: (x[0] * y[0] - x[1] * y[1], x[0] * y[1] + x[1] * y[0])
    a1 = (flat(ar), flat(ai))
    a2 = cmul(a1, a1)
    a4 = cmul(a2, a2)
    step = jnp.concatenate([a1[0], a1[1], a2[0], a2[1], a4[0], a4[1], jnp.zeros((2, g * p), F32)], axis=0)
    pows = [a1]
    for _ in range(SUBLANES - 1):
        pows.append(cmul(pows[-1], a1))
    powers = jnp.stack([jnp.concatenate([q[0] for q in pows], axis=0), jnp.concatenate([q[1] for q in pows], axis=0)])
    return win, step, powers, cout, d.reshape(1, g * hw).astype(F32), w_glu.astype(MXU_DTYPE)


def _s5(u, weights, batch, seq):
    win, step, powers, cout, d, wglu = weights
    tb = min(S5_TB, seq)
    ns = S5_GROUPS * S5_STATE
    width = S5_GROUPS * S5_GROUP_WIDTH
    u3 = u.reshape(batch, seq, width)
    blk = pl.BlockSpec((1, tb, width), lambda b, i: (b, i, 0))
    return pl.pallas_call(
        _s5_kernel,
        grid=(batch, seq // tb),
        in_specs=[blk] + [_const_spec(a.shape) for a in (win, step, powers, cout, d, wglu)],
        out_specs=blk,
        out_shape=jax.ShapeDtypeStruct((batch, seq, width), F32),
        scratch_shapes=[pltpu.VMEM((tb, ns), F32), pltpu.VMEM((tb, ns), F32), pltpu.VMEM((SUBLANES, ns), F32)],
        compiler_params=_cparams("parallel", "arbitrary"),
    )(u3, win, step, powers, cout, d, wglu)


def _merge_kernel(x_ref, yn_ref, yr_ref, ys_ref, mg_ref, wun_ref, wur_ref, wus_ref, wout_ref, g_ref, o_ref):
    d = x_ref.shape[1]
    merged = (mg_ref[:, 0:d] * _dot(yn_ref[...], wun_ref[...]) + mg_ref[:, d:2 * d] * _dot(yr_ref[...], wur_ref[...])
              + mg_ref[:, 2 * d:3 * d] * _dot(ys_ref[...], wus_ref[...]))
    o_ref[...] = x_ref[...] + _rms(_dot(merged, wout_ref[...]), g_ref[...])


def _merge(x2, y_nsa, y_rwkv, y_s5, mg, wun, wur, wus, wout, gain, tm):
    m, d = x2.shape
    row = lambda a: pl.BlockSpec((tm, a.shape[1]), lambda i: (i, 0))
    return pl.pallas_call(
        _merge_kernel,
        grid=(m // tm,),
        in_specs=[row(x2), row(y_nsa), row(y_rwkv), row(y_s5), row(mg)]
        + [_const_spec(a.shape) for a in (wun, wur, wus, wout, gain)],
        out_specs=row(x2),
        out_shape=jax.ShapeDtypeStruct((m, d), F32),
        compiler_params=_cparams("parallel"),
    )(x2, y_nsa, y_rwkv, y_s5, mg, wun, wur, wus, wout, gain)


def _memkv_kernel(m_ref, g_ref, w_ref, o_ref):
    o_ref[0] = _dot(_rms(m_ref[0], g_ref[...]), w_ref[...]).astype(o_ref.dtype)


def _memkv(mem, gain, wkv):
    b, n, d = mem.shape
    return pl.pallas_call(
        _memkv_kernel,
        grid=(b,),
        in_specs=[pl.BlockSpec((1, n, d), lambda i: (i, 0, 0)), _const_spec(gain.shape), _const_spec(wkv.shape)],
        out_specs=pl.BlockSpec((1, n, wkv.shape[1]), lambda i: (i, 0, 0)),
        out_shape=jax.ShapeDtypeStruct((b, n, wkv.shape[1]), MXU_DTYPE),
        compiler_params=_cparams("parallel"),
    )(mem, gain, wkv)


def _xattn_kernel(x_ref, kv_ref, gin_ref, gout_ref, wq_ref, wo_ref, o_ref):
    x = x_ref[0]
    width = XA_HEADS * XA_HEAD_DIM
    q = (_dot(_rms(x, gin_ref[...]), wq_ref[...]) * (XA_HEAD_DIM ** -0.5)).astype(MXU_DTYPE)
    k = kv_ref[0, :, 0:width]
    v = kv_ref[0, :, width:2 * width]
    head = lax.broadcasted_iota(jnp.int32, q.shape, 1) // XA_HEAD_DIM
    att = jnp.zeros(q.shape, F32)
    for h in range(XA_HEADS):
        mine = head == h
        p = _softmax_last(_dot_nt(jnp.where(mine, q, jnp.zeros_like(q)), k))
        att = att + jnp.where(mine, _dot(p, v), 0.0)
    o_ref[0] = x + _rms(_dot(att, wo_ref[...]), gout_ref[...])


def _xattn(x3, kv, gin, gout, wq, wo, tm):
    b, s, d = x3.shape
    blk = pl.BlockSpec((1, tm, d), lambda i, j: (i, j, 0))
    return pl.pallas_call(
        _xattn_kernel,
        grid=(b, s // tm),
        in_specs=[blk, pl.BlockSpec((1,) + kv.shape[1:], lambda i, j: (i, 0, 0))]
        + [_const_spec(a.shape) for a in (gin, gout, wq, wo)],
        out_specs=blk,
        out_shape=jax.ShapeDtypeStruct(x3.shape, F32),
        compiler_params=_cparams("parallel", "parallel"),
    )(x3, kv, gin, gout, wq, wo)


FFN_CHUNK = 256


def _ffn_kernel(x_ref, gin_ref, gout_ref, wg_ref, wu_ref, wd_ref, o_ref):
    x = x_ref[...]
    h = _rms(x, gin_ref[...]).astype(MXU_DTYPE)
    acc = jnp.zeros(x.shape, F32)
    for c in range(wg_ref.shape[1] // FFN_CHUNK):
        sl = slice(c * FFN_CHUNK, (c + 1) * FFN_CHUNK)
        a = jnp.dot(h, wg_ref[:, sl], preferred_element_type=F32)
        b = jnp.dot(h, wu_ref[:, sl], preferred_element_type=F32)
        acc = acc + _dot(a * _sigmoid(a) * b, wd_ref[sl, :])
    o_ref[...] = x + _rms(acc, gout_ref[...])


def _ffn(x2, gin, gout, wg, wu, wd, tm):
    m, d = x2.shape
    assert wg.shape[1] % FFN_CHUNK == 0
    row = pl.BlockSpec((tm, d), lambda i: (i, 0))
    return pl.pallas_call(
        _ffn_kernel,
        grid=(m // tm,),
        in_specs=[row] + [_const_spec(a.shape) for a in (gin, gout, wg, wu, wd)],
        out_specs=row,
        out_shape=jax.ShapeDtypeStruct((m, d), F32),
        compiler_params=_cparams("parallel"),
    )(x2, gin, gout, wg, wu, wd)


ROW_TILE = 256


def kernel(x, mem, norm_gains, mem_norm, w_in, nsa_cmp_pos_k, nsa_cmp_pos_v, nsa_ck_w1, nsa_ck_w2, nsa_cv_w1, nsa_cv_w2, rwkv_mu, rwkv_w0, rwkv_w2, rwkv_a0, rwkv_a2, rwkv_g2, rwkv_k_k, rwkv_k_a, rwkv_r_k, rwkv_ln_w, rwkv_ln_b, s5_lam_re, s5_lam_im, s5_log_dt, s5_b_re, s5_b_im, s5_c_re, s5_c_im, s5_d, s5_w_glu, w_up_nsa, w_up_rwkv, w_up_s5, w_out, xa_w_q, xa_w_k, xa_w_v, xa_w_o, ffn_w_gate, ffn_w_up, ffn_w_down):
    batch, seq, d = x.shape
    depth = w_in.shape[0]
    tm = min(ROW_TILE, seq)
    bf = lambda a: a.astype(MXU_DTYPE)
    gain = lambda l, i: norm_gains[l, i].reshape(1, d)

    half = NSA_HEAD_DIM // 2
    inv = 1.0 / (ROPE_THETA ** (jnp.arange(0, NSA_HEAD_DIM, 2, dtype=F32) / NSA_HEAD_DIM))
    ang = jnp.arange(seq, dtype=F32)[:, None] * inv[None, :]
    cos, sin = jnp.cos(ang), jnp.sin(ang)
    cos128 = jnp.tile(cos, (1, LANES // half))
    sin128 = jnp.tile(jnp.concatenate([-sin, sin], axis=1), (1, LANES // NSA_HEAD_DIM))

    x2 = x.reshape(batch * seq, d)
    for l in range(depth):
        z = _inproj(x2, gain(l, 0), cos128, sin128, _regroup_w_in(w_in[l]), seq, tm)

        chunks = lambda a: a.reshape(batch, seq // NSA_CMP_STRIDE, NSA_CMP_STRIDE * a.shape[-1])
        kc2, vc2 = _compress(chunks(z["kc"]), chunks(z["vc"]),
                             *_compress_weights(nsa_cmp_pos_k[l], nsa_ck_w1[l], nsa_ck_w2[l]),
                             *_compress_weights(nsa_cmp_pos_v[l], nsa_cv_w1[l], nsa_cv_w2[l]))
        y_nsa = _nsa(z, kc2, vc2, batch, seq)

        rank = rwkv_w2.shape[1]
        width = RWKV_HEADS * RWKV_HEAD_DIM
        zero = jnp.zeros((rank, width), F32)
        wa2 = bf(jnp.stack([jnp.concatenate([rwkv_w2[l], zero], axis=0), jnp.concatenate([zero, rwkv_a2[l]], axis=0)]))
        vecs = jnp.stack([rwkv_w0[l], rwkv_a0[l], rwkv_k_k[l], rwkv_k_a[l], rwkv_r_k[l].reshape(width), rwkv_ln_w[l],
                          rwkv_ln_b[l], jnp.zeros((width,), F32)])
        y_rwkv = _rwkv(z["rw"], rwkv_mu[l].reshape(1, -1), vecs, wa2, bf(rwkv_g2[l]), batch, seq)

        y_s5 = _s5(z["s5"], _s5_weights(s5_lam_re[l], s5_lam_im[l], s5_log_dt[l], s5_b_re[l], s5_b_im[l], s5_c_re[l],
                                        s5_c_im[l], s5_d[l], s5_w_glu[l]), batch, seq)

        flat = lambda a: a.reshape(batch * seq, a.shape[-1])
        x2 = _merge(x2, flat(y_nsa), flat(y_rwkv), flat(y_s5), z["mg"], bf(w_up_nsa[l]), bf(w_up_rwkv[l]),
                    bf(w_up_s5[l]), bf(w_out[l]), gain(l, 1), tm)

        kv = _memkv(mem, mem_norm[l].reshape(1, d), bf(jnp.concatenate([xa_w_k[l], xa_w_v[l]], axis=1)))
        x2 = _xattn(x2.reshape(batch, seq, d), kv, gain(l, 2), gain(l, 3), bf(xa_w_q[l]), bf(xa_w_o[l]),
                    tm).reshape(batch * seq, d)

        x2 = _ffn(x2, gain(l, 4), gain(l, 5), bf(ffn_w_gate[l]), bf(ffn_w_up[l]), bf(ffn_w_down[l]), tm)
    return x2.reshape(batch, seq, d)
```

```python
import functools
import math

import jax
import jax.numpy as jnp
import numpy as np
from jax import lax
from jax.experimental import pallas as pl
from jax.experimental.pallas import tpu as pltpu

NSA_HEADS = 8
NSA_KV_GROUPS = 2
NSA_HEAD_DIM = 64
NSA_CMP_LEN = 32
NSA_CMP_STRIDE = 16
NSA_SEL_LEN = 64
NSA_SEL_TOPK = 16
NSA_WINDOW = 512
RWKV_HEADS = 4
RWKV_HEAD_DIM = 64
RWKV_LN_EPS = 64e-5
S5_GROUPS = 16
S5_GROUP_WIDTH = 16
S5_STATE = 64
XA_HEADS = 4
XA_HEAD_DIM = 64
ROPE_THETA = 10000.0
NORM_EPS = 1e-6
NEG_INF = -1e30
SEL_FORCE_SCORE = 1e9

LANES = 128
SUBLANES = 8
MXU_DTYPE = jnp.bfloat16
VMEM_LIMIT = 56 << 20

F32 = jnp.float32


def _cparams(*sem):
    return pltpu.CompilerParams(dimension_semantics=sem, vmem_limit_bytes=VMEM_LIMIT)


class _Of:
    def __init__(self, arr, *idx):
        self.arr, self.idx = arr, idx

    @property
    def shape(self):
        return self.arr.shape[len(self.idx):]


def _arr(w):
    return w.arr if isinstance(w, _Of) else w


def _spec(w):
    nd = len(w.shape)
    if isinstance(w, _Of):
        lead = tuple(w.idx)
        return pl.BlockSpec((None,) * len(lead) + tuple(w.shape), lambda *_: lead + (0,) * nd,
                            pipeline_mode=pl.Buffered(1))
    return pl.BlockSpec(w.shape, lambda *_: (0,) * nd, pipeline_mode=pl.Buffered(1))


def _dot(a, b):
    return jnp.dot(a.astype(MXU_DTYPE), b.astype(MXU_DTYPE), preferred_element_type=F32)


def _dot_nt(a, b):
    return lax.dot_general(a.astype(MXU_DTYPE), b.astype(MXU_DTYPE), (((1,), (1,)), ((), ())),
                           preferred_element_type=F32)


def _split(a, terms):
    parts = []
    for _ in range(terms - 1):
        p = a.astype(MXU_DTYPE)
        parts.append(p)
        a = a - p.astype(F32)
    parts.append(a.astype(MXU_DTYPE))
    return parts


def _dot_split_lhs(a, b01, terms=3):
    b01 = b01.astype(MXU_DTYPE)
    return sum(jnp.dot(p, b01, preferred_element_type=F32) for p in _split(a, terms))


def _dot_split_rhs(a01, b, terms=3):
    a01 = a01.astype(MXU_DTYPE)
    return sum(jnp.dot(a01, p, preferred_element_type=F32) for p in _split(b, terms))


def _rms(x, g):
    return x * lax.rsqrt(jnp.mean(x * x, axis=-1, keepdims=True) + NORM_EPS) * g


def _sigmoid(x):
    return 1.0 / (1.0 + jnp.exp(-x))


def _gelu(x):
    return 0.5 * x * (1.0 + jnp.tanh(math.sqrt(2.0 / math.pi) * (x + 0.044715 * (x * x * x))))


def _softmax_last(s):
    m = jnp.max(s, axis=-1, keepdims=True)
    e = jnp.exp(s - m)
    return e / jnp.sum(e, axis=-1, keepdims=True)


_SEG = dict(q=(0, 512), kc=(512, 640), vc=(640, 768), ks=(768, 1024), vs=(1024, 1280), kw=(1280, 1536),
            vw=(1536, 1792), ng=(1792, 1920), rw=(1920, 2944), s5=(2944, 3200), mg=(3200, 6272))
_W_IN_COLS = 6272


def _rope128(x, cos, sin_signed):
    lane = lax.broadcasted_iota(jnp.int32, x.shape, 1)
    first = (lane % NSA_HEAD_DIM) < (NSA_HEAD_DIM // 2)
    rot = jnp.where(first, pltpu.roll(x, LANES - NSA_HEAD_DIM // 2, 1), pltpu.roll(x, NSA_HEAD_DIM // 2, 1))
    return x * cos + rot * sin_signed


def _inproj_kernel(x_ref, g_ref, cos_ref, sin_ref, w_ref,
                   qc_o, qr_o, kc_o, vc_o, ks_o, vs_o, kw_o, vw_o, ng_o, rw_o, s5_o, mg_o, *, n_tab):
    h = _rms(x_ref[...], g_ref[...]).astype(MXU_DTYPE)
    cos = cos_ref[...]
    sin = sin_ref[...]
    scale = NSA_HEAD_DIM ** -0.5 * math.log2(math.e)

    def seg(name):
        a, b = _SEG[name]
        return jnp.dot(h, w_ref[:, a:b], preferred_element_type=F32)

    q = seg("q")
    qc_o[...] = (q * scale).astype(qc_o.dtype)
    for c in range(q.shape[1] // LANES):
        sl = slice(c * LANES, (c + 1) * LANES)
        qr_o[:, sl] = (_rope128(q[:, sl], cos, sin) * scale).astype(qr_o.dtype)
    kc_o[...] = seg("kc")
    vc_o[...] = seg("vc")
    k = seg("kw")
    for c in range(NSA_KV_GROUPS):
        sl = slice(c * LANES, (c + 1) * LANES)
        kw_o[:, sl] = _rope128(k[:, sl], cos, sin).astype(kw_o.dtype)
    tm = x_ref.shape[0]
    tok = (pl.program_id(0) % n_tab) * tm + lax.broadcasted_iota(jnp.int32, (tm, LANES), 0)
    onehot = jnp.where(lax.broadcasted_iota(jnp.int32, (tm, LANES), 1) == tok // NSA_SEL_LEN, 1.0, 0.0)
    k = seg("ks")
    for c in range(NSA_KV_GROUPS):
        ks_o[:, 2 * c * LANES:(2 * c + 1) * LANES] = _rope128(k[:, c * LANES:(c + 1) * LANES], cos, sin).astype(ks_o.dtype)
        ks_o[:, (2 * c + 1) * LANES:(2 * c + 2) * LANES] = onehot.astype(ks_o.dtype)
    ones = jnp.ones((tm, LANES), vs_o.dtype)
    for name, out in (("vs", vs_o), ("vw", vw_o)):
        val = seg(name)
        for c in range(NSA_KV_GROUPS):
            out[:, 2 * c * LANES:(2 * c + 1) * LANES] = val[:, c * LANES:(c + 1) * LANES].astype(out.dtype)
            out[:, (2 * c + 1) * LANES:(2 * c + 2) * LANES] = ones
    ng_o[...] = _sigmoid(seg("ng"))
    rw_o[...] = seg("rw")
    s5_o[...] = seg("s5")
    mg_o[...] = _sigmoid(seg("mg"))


def _dup_groups(w):
    hd = NSA_HEAD_DIM
    return jnp.concatenate([w[:, :hd], w[:, :hd], w[:, hd:], w[:, hd:]], axis=1)


def _regroup_w_in(w):
    d = w.shape[0]
    q_w = NSA_HEADS * NSA_HEAD_DIM
    kv_w = NSA_KV_GROUPS * NSA_HEAD_DIM
    o = 0
    wq = w[:, o:o + q_w]; o += q_w
    kv = [w[:, o + i * kv_w:o + (i + 1) * kv_w] for i in range(6)]; o += 6 * kv_w
    wng = w[:, o:o + 3 * NSA_HEADS]; o += 3 * NSA_HEADS
    rw_w = 3 * RWKV_HEADS * RWKV_HEAD_DIM + 256
    wrw = w[:, o:o + rw_w]; o += rw_w
    s5_w = S5_GROUPS * S5_GROUP_WIDTH
    ws5 = w[:, o:o + s5_w]; o += s5_w
    wmg = w[:, o:]
    wng = jnp.pad(wng, ((0, 0), (0, LANES - wng.shape[1])))
    out = jnp.concatenate([wq, kv[0], kv[1], _dup_groups(kv[2]), _dup_groups(kv[3]), _dup_groups(kv[4]),
                           _dup_groups(kv[5]), wng, wrw, ws5, wmg], axis=1)
    assert out.shape == (d, _W_IN_COLS), out.shape
    return out.astype(MXU_DTYPE)


def _inproj(x2, gain, cos128, sin128, w_all, seq, tm):
    m, d = x2.shape
    n_tab = seq // tm
    row = lambda w: pl.BlockSpec((tm, w), lambda i: (i, 0))
    tab = pl.BlockSpec((tm, LANES), lambda i: (i % n_tab, 0))
    widths = dict(qc=512, qr=512, kc=128, vc=128, ks=512, vs=512, kw=256, vw=512, ng=128, rw=1024, s5=256, mg=3072)
    dtypes = dict(qc=MXU_DTYPE, qr=MXU_DTYPE, kc=F32, vc=F32, ks=MXU_DTYPE, vs=MXU_DTYPE, kw=MXU_DTYPE,
                  vw=MXU_DTYPE, ng=F32, rw=F32, s5=F32, mg=F32)
    names = list(widths)
    outs = pl.pallas_call(
        functools.partial(_inproj_kernel, n_tab=n_tab),
        grid=(m // tm,),
        in_specs=[row(d), _spec(gain), tab, tab, _spec(w_all)],
        out_specs=[row(widths[n]) for n in names],
        out_shape=[jax.ShapeDtypeStruct((m, widths[n]), dtypes[n]) for n in names],
        compiler_params=_cparams("parallel"),
    )(x2, _arr(gain), cos128, sin128, _arr(w_all))
    return dict(zip(names, outs))


def _compress_kernel(k_ref, v_ref, pk_ref, pv_ref, kw1_ref, kw2_ref, vw1_ref, vw2_ref, kc_o, vc_o):
    def one(x_ref, p_ref, w1_ref, w2_ref):
        x = x_ref[0]
        n = x.shape[0]
        top = _dot(x + p_ref[0:1, :], w1_ref[0])
        bot = _dot(x + p_ref[1:2, :], w1_ref[1])
        hid = top + pltpu.roll(bot, n - 1, 0)
        return _dot(_gelu(hid), w2_ref[...])

    kc_o[0] = one(k_ref, pk_ref, kw1_ref, kw2_ref).astype(kc_o.dtype)
    vc = one(v_ref, pv_ref, vw1_ref, vw2_ref)
    for c in range(NSA_KV_GROUPS):
        vc_o[0, :, 2 * c * LANES:(2 * c + 1) * LANES] = vc[:, c * LANES:(c + 1) * LANES].astype(vc_o.dtype)
        vc_o[0, :, (2 * c + 1) * LANES:(2 * c + 2) * LANES] = jnp.ones((vc.shape[0], LANES), vc_o.dtype)


def _compress_weights(pos, w1, w2):
    g, hd, half = NSA_KV_GROUPS, NSA_HEAD_DIM, NSA_CMP_STRIDE
    hidden = w1.shape[1]
    eye = jnp.eye(g, dtype=w1.dtype)
    w1r = w1.reshape(2, half, hd, hidden)
    w1e = jnp.einsum("pldj,ab->pladbj", w1r, eye).reshape(2, half * g * hd, g * hidden)
    w2d = jnp.concatenate([w2, w2], axis=1)
    w2e = jnp.einsum("jd,ab->ajbd", w2d, eye).reshape(g * hidden, g * 2 * hd)
    pos_e = jnp.broadcast_to(pos.reshape(2, half, 1, hd), (2, half, g, hd)).reshape(2, half * g * hd)
    return pos_e.astype(F32), w1e.astype(MXU_DTYPE), w2e.astype(MXU_DTYPE)


def _compress(kc, vc, pk, kw1, kw2, pv, vw1, vw2):
    b, n, width = kc.shape
    blk = pl.BlockSpec((1, n, width), lambda i: (i, 0, 0))
    out_w = kw2.shape[1]
    out_blk = lambda w: pl.BlockSpec((1, n, w), lambda i: (i, 0, 0))
    return pl.pallas_call(
        _compress_kernel,
        grid=(b,),
        in_specs=[blk, blk] + [_spec(w) for w in (pk, pv, kw1, kw2, vw1, vw2)],
        out_specs=[out_blk(out_w), out_blk(2 * out_w)],
        out_shape=[jax.ShapeDtypeStruct((b, n, out_w), MXU_DTYPE), jax.ShapeDtypeStruct((b, n, 2 * out_w), MXU_DTYPE)],
        compiler_params=_cparams("parallel"),
    )(kc, vc, *map(_arr, (pk, pv, kw1, kw2, vw1, vw2)))


NSA_TQ = 128
NSA_TK = 1024
MASKED = -1e30


def _nsa_kernel(qc_ref, qr_ref, kc_ref, vc_ref, ks_ref, vs_ref, kw_ref, vw_ref, ng_ref, ovl_ref, gexp_ref, y_ref, *, top_k):
    tq = qc_ref.shape[1]
    seq = ks_ref.shape[1]
    n_cmp_pad = kc_ref.shape[1]
    hpg = NSA_HEADS // NSA_KV_GROUPS
    rows = hpg * tq
    n_sel = seq // NSA_SEL_LEN
    n_sel_pad = -(-n_sel // SUBLANES) * SUBLANES
    t0 = pl.program_id(1) * tq
    tcol = t0 + lax.broadcasted_iota(jnp.int32, (tq, 1), 0)
    low_half = lax.broadcasted_iota(jnp.int32, (tq, LANES), 1) < NSA_HEAD_DIM

    def stack_heads(q_ref, g):
        parts = []
        for r in range(hpg):
            h = g * hpg + r
            slab = q_ref[0, :, (h // 2) * LANES:(h // 2 + 1) * LANES]
            keep = low_half if h % 2 == 0 else jnp.logical_not(low_half)
            parts.append(jnp.where(keep, slab, jnp.zeros_like(slab)))
        return jnp.concatenate(parts, axis=0)

    def unstack(o, g, y_parts):
        for pair in range(hpg // 2):
            even = o[(2 * pair) * tq:(2 * pair + 1) * tq]
            odd = o[(2 * pair + 1) * tq:(2 * pair + 2) * tq]
            y_parts[g * (hpg // 2) + pair] = jnp.where(low_half, even, odd)

    def masked_exp_pv(q, k, bias, v_ones):
        s = [_dot_nt(q[g], k[g]) for g in groups]
        n = s[0].shape[-1]
        s = [sg.reshape(hpg, tq, n) + bias[None] for sg in s]
        e = [jnp.exp2(sg - jnp.max(sg, axis=-1, keepdims=True)) for sg in s]
        return e, [_dot(e[g].reshape(rows, n), v_ones[g]) for g in groups]

    cmp_end = lax.broadcasted_iota(jnp.int32, (1, n_cmp_pad), 1) * NSA_CMP_STRIDE + (NSA_CMP_LEN - 1)
    cmp_bias = jnp.where(cmp_end <= tcol, 0.0, NEG_INF)
    cmp_live = jnp.where(tcol >= NSA_CMP_LEN - 1, 1.0, 0.0)
    slab = NSA_WINDOW + tq
    w0 = pl.multiple_of(jnp.maximum(t0 - NSA_WINDOW, 0), tq)
    wpos = w0 + lax.broadcasted_iota(jnp.int32, (1, slab), 1)
    win_bias = jnp.where(wpos <= tcol, jnp.where(wpos > tcol - NSA_WINDOW, 0.0, NEG_INF), NEG_INF)
    n_kt = (t0 + tq - 1) // NSA_TK + 1
    k_last = pl.multiple_of((n_kt - 1) * NSA_TK, NSA_TK)
    diag_bias = jnp.where(k_last + lax.broadcasted_iota(jnp.int32, (1, NSA_TK), 1) <= tcol, 0.0, NEG_INF)

    blk = lax.broadcasted_iota(jnp.int32, (n_sel_pad, tq), 0)
    cur = (t0 + lax.broadcasted_iota(jnp.int32, (n_sel_pad, tq), 1)) // NSA_SEL_LEN
    forced = (blk == 0) | (blk == cur) | (blk == cur - 1)

    groups = range(NSA_KV_GROUPS)
    gl = [slice(g * LANES, (g + 1) * LANES) for g in groups]
    gx = [slice(2 * g * LANES, (2 * g + 2) * LANES) for g in groups]
    y_cmp, y_sel, y_win = [None] * 4, [None] * 4, [None] * 4
    qc = [stack_heads(qc_ref, g) for g in groups]
    qr = [stack_heads(qr_ref, g) for g in groups]

    cmp_e, cmp_pv = masked_exp_pv(qc, [kc_ref[0, :, gl[g]] for g in groups], cmp_bias, [vc_ref[0, :, gx[g]] for g in groups])
    cmp_inv = [(cmp_live[None] / cmp_pv[g][:, LANES:].reshape(hpg, tq, LANES)) for g in groups]
    for g in groups:
        unstack(cmp_pv[g][:, :LANES] * cmp_inv[g].reshape(rows, LANES), g, y_cmp)

    _, win = masked_exp_pv(qr, [kw_ref[0, pl.ds(w0, slab), gl[g]] for g in groups], win_bias,
                           [vw_ref[0, pl.ds(w0, slab), gx[g]] for g in groups])
    for g in groups:
        unstack(win[g][:, :LANES] / win[g][:, LANES:], g, y_win)

    qx = []
    for g in groups:
        p = cmp_e[g] * jnp.concatenate([cmp_inv[g]] * -(-n_cmp_pad // LANES), axis=-1)[..., :n_cmp_pad]
        imp = _dot_split_lhs(jnp.sum(p, axis=0), ovl_ref[...])
        vals = jnp.where(blk <= cur, jnp.where(forced, SEL_FORCE_SCORE, imp.T[:n_sel_pad]), NEG_INF)
        rank = [jnp.zeros((SUBLANES, tq), F32) for _ in range(n_sel_pad // SUBLANES)]
        for i in range(n_sel):
            other = vals[i:i + 1, :]
            for u in range(n_sel_pad // SUBLANES):
                mine = vals[u * SUBLANES:(u + 1) * SUBLANES]
                if i < u * SUBLANES:
                    ahead = other >= mine
                elif i >= (u + 1) * SUBLANES:
                    ahead = other > mine
                else:
                    ahead = (other > mine) | ((other == mine) & (blk[u * SUBLANES:(u + 1) * SUBLANES] > i))
                rank[u] = rank[u] + jnp.where(ahead, 1.0, 0.0)
        drop = jnp.where(jnp.concatenate(rank, axis=0) < top_k, 0.0, MASKED)
        drop = jnp.concatenate([drop, jnp.zeros((LANES - n_sel_pad, tq), F32)], axis=0).T.astype(MXU_DTYPE)
        qx.append(jnp.concatenate([qr[g], jnp.concatenate([drop] * hpg, axis=0)], axis=1))

    def sel_tiles(k0, bias, carries):
        s = [_dot_nt(qx[g], ks_ref[0, pl.ds(k0, NSA_TK), gx[g]]).reshape(hpg, tq, NSA_TK) for g in groups]
        if bias is not None:
            s = [sg + bias[None] for sg in s]
        m_new = [jnp.maximum(carries[g][0], jnp.max(s[g], axis=-1, keepdims=True)) for g in groups]
        e = [jnp.exp2(s[g] - m_new[g]) for g in groups]
        pv = [_dot(e[g].reshape(rows, NSA_TK), vs_ref[0, pl.ds(k0, NSA_TK), gx[g]]) for g in groups]
        return tuple((m_new[g], jnp.exp2(carries[g][0] - m_new[g]) * carries[g][1] + pv[g].reshape(hpg, tq, 2 * LANES))
                     for g in groups)

    init = (jnp.full((hpg, tq, 1), NEG_INF, F32), jnp.zeros((hpg, tq, 2 * LANES), F32))
    carries = lax.fori_loop(0, n_kt - 1, lambda j, c: sel_tiles(pl.multiple_of(j * NSA_TK, NSA_TK), None, c),
                            tuple(init for _ in groups))
    carries = sel_tiles(k_last, diag_bias, carries)
    for g in groups:
        acc = carries[g][1].reshape(rows, 2 * LANES)
        unstack(acc[:, :LANES] / acc[:, LANES:], g, y_sel)

    gates = ng_ref[0]
    q_w = NSA_HEADS * NSA_HEAD_DIM
    spread = _dot_split_lhs(gates, gexp_ref[...], terms=2)
    for c in range(4):
        sl = slice(c * LANES, (c + 1) * LANES)
        y_ref[0, :, sl] = (spread[:, sl] * y_cmp[c] + spread[:, q_w + c * LANES:q_w + (c + 1) * LANES] * y_sel[c]
                           + spread[:, 2 * q_w + c * LANES:2 * q_w + (c + 1) * LANES] * y_win[c])


def _nsa_constants(seq, n_cmp_pad):
    n_sel = seq // NSA_SEL_LEN
    cmp_start = np.arange(n_cmp_pad) * NSA_CMP_STRIDE
    sel_start = np.arange(LANES) * NSA_SEL_LEN
    ovl = ((cmp_start[:, None] < sel_start[None, :] + NSA_SEL_LEN) & (cmp_start[:, None] + NSA_CMP_LEN > sel_start[None, :])
           & (np.arange(LANES)[None, :] < n_sel) & (np.arange(n_cmp_pad)[:, None] < n_cmp_pad - 1))
    q_w = NSA_HEADS * NSA_HEAD_DIM
    gexp = np.zeros((LANES, 3 * q_w), np.float32)
    for i in range(3):
        for h in range(NSA_HEADS):
            gexp[3 * h + i, i * q_w + h * NSA_HEAD_DIM:i * q_w + (h + 1) * NSA_HEAD_DIM] = 1.0
    return jnp.asarray(ovl.astype(np.float32), dtype=MXU_DTYPE), jnp.asarray(gexp, dtype=MXU_DTYPE)


def _nsa(z, kc2, vc2, batch, seq):
    tq = NSA_TQ
    q_w = NSA_HEADS * NSA_HEAD_DIM
    n_cmp_pad = kc2.shape[1]
    ovl, gexp = _nsa_constants(seq, n_cmp_pad)
    r3 = lambda a: a.reshape(batch, seq, a.shape[-1])
    qblk = pl.BlockSpec((1, tq, q_w), lambda b, i: (b, i, 0))
    full = lambda n, w: pl.BlockSpec((1, n, w), lambda b, i: (b, 0, 0))
    top_k = min(NSA_SEL_TOPK, seq // NSA_SEL_LEN)
    return pl.pallas_call(
        functools.partial(_nsa_kernel, top_k=top_k),
        grid=(batch, seq // tq),
        in_specs=[qblk, qblk, full(n_cmp_pad, 256), full(n_cmp_pad, 512), full(seq, 512), full(seq, 512),
                  full(seq, 256), full(seq, 512), pl.BlockSpec((1, tq, LANES), lambda b, i: (b, i, 0)),
                  _spec(ovl), _spec(gexp)],
        out_specs=qblk,
        out_shape=jax.ShapeDtypeStruct((batch, seq, q_w), F32),
        compiler_params=_cparams("parallel", "arbitrary"),
    )(r3(z["qc"]), r3(z["qr"]), kc2, vc2, r3(z["ks"]), r3(z["vs"]), r3(z["kw"]), r3(z["vw"]), r3(z["ng"]),
      ovl, gexp)


RWKV_CHUNK = 64
RWKV_TB = 256


def _rwkv_kernel(z_ref, mu_ref, vecs_ref, wa2_ref, g2_ref, ones_ref, o_ref, prev_ref, st_ref):
    tb = z_ref.shape[1]
    width = RWKV_HEADS * RWKV_HEAD_DIM
    n = RWKV_HEAD_DIM
    c = RWKV_CHUNK

    @pl.when(pl.program_id(1) == 0)
    def _():
        prev_ref[...] = jnp.zeros_like(prev_ref)
        st_ref[...] = jnp.zeros_like(st_ref)

    z = z_ref[0]
    row = lax.broadcasted_iota(jnp.int32, (tb, 1), 0)
    z_prev = jnp.where(row == 0, prev_ref[...], pltpu.roll(z, 1, 0))
    prev_ref[...] = z[tb - 1:tb, :]
    zs = z + (z_prev - z) * mu_ref[...]

    w0, a0, k_k, k_a, r_k, ln_w, ln_b = (vecs_ref[i:i + 1, :] for i in range(7))
    r = zs[:, 0:width]
    k = zs[:, width:2 * width]
    v = zs[:, 2 * width:3 * width]
    lr = zs[:, 3 * width:3 * width + LANES]
    w = w0 + _dot(jnp.tanh(lr), wa2_ref[0])
    w = -(jnp.maximum(-w, 0.0) + jnp.log(1.0 + jnp.exp(-jnp.abs(w)))) - 0.5
    logd = -jnp.exp(w)
    a = _sigmoid(a0 + _dot(lr, wa2_ref[1]))
    gate = _dot(_sigmoid(zs[:, 3 * width + LANES:]), g2_ref[...])
    ones_h = ones_ref[...]
    kk = k * k_k
    kk = kk / jnp.maximum(jnp.sqrt(_dot_split_lhs(kk * kk, ones_h, terms=2)), 1e-12)
    k = k * (1.0 + (a - 1.0) * k_a)
    bonus = _dot_split_lhs(r * k * r_k, ones_h, terms=2) * v

    ri = lax.broadcasted_iota(jnp.int32, (tb, tb), 0)
    ci = lax.broadcasted_iota(jnp.int32, (tb, tb), 1)
    same = (ri // c) == (ci // c)
    cl = _dot_split_rhs(jnp.where(same & (ci <= ri), 1.0, 0.0), logd)
    cend = jnp.concatenate([jnp.broadcast_to(cl[q * c + c - 1:q * c + c, :], (c, width)) for q in range(tb // c)],
                           axis=0)
    e_inv = jnp.exp(-cl)
    e_end = jnp.exp(cend - cl)
    a_t = -kk * jnp.exp(cl - logd)
    r_t = r * jnp.exp(cl)
    b = kk * a
    b_t = b * e_inv
    k_t = k * e_inv
    b_end_T = (b * e_end).T
    k_end_T = (k * e_end).T
    w_end = jnp.exp(cend)

    head_of_row = lax.broadcasted_iota(jnp.int32, (width, width), 0) // n
    head_of_lane = lax.broadcasted_iota(jnp.int32, (width, width), 1) // n
    on_diag_block = head_of_row == head_of_lane
    eye_w = lax.broadcasted_iota(jnp.int32, (width, width), 0) == lax.broadcasted_iota(jnp.int32, (width, width), 1)
    step = lax.broadcasted_iota(jnp.int32, (c, width), 0)
    within = lax.broadcasted_iota(jnp.int32, (c, width), 1) % n
    strict = within < step
    incl = within <= step
    eye_c = jnp.where(within == step, 1.0, 0.0)

    def bd(m):
        m = m.astype(MXU_DTYPE)
        return jnp.where(on_diag_block, jnp.concatenate([m] * RWKV_HEADS, axis=0), jnp.zeros((), MXU_DTYPE))

    chunk_ids = range(tb // c)
    qs = [slice(q * c, (q + 1) * c) for q in chunk_ids]
    xs = [jnp.concatenate([a_t[s], r_t[s]], axis=0) for s in qs]
    pb = [_dot_nt(xs[q], bd(b_t[qs[q]])) for q in chunk_ids]
    pk = [_dot_nt(xs[q], bd(k_t[qs[q]])) for q in chunk_ids]
    n_low = [jnp.where(strict, p[:c], 0.0) for p in pb]
    t_inv = [eye_c + jnp.where((step // 2) == (within // 2), m, 0.0) for m in n_low]
    size = 2
    while size < c:
        off = ((step // (2 * size)) == (within // (2 * size))) & ((step // size) != (within // size))
        y = [_dot(jnp.where(off, m, 0.0), bd(t)) for m, t in zip(n_low, t_inv)]
        t_inv = [t + _dot(t, bd(yy)) for t, yy in zip(t_inv, y)]
        size *= 2
    v_bd = [bd(v[s]) for s in qs]
    nv = [_dot(jnp.where(strict, pk[q][:c], 0.0), v_bd[q]) for q in chunk_ids]
    mg1 = [_dot(t_inv[q], jnp.concatenate([bd(a_t[qs[q]]), bd(nv[q])], axis=1)) for q in chunk_ids]
    dm = [_dot(jnp.where(incl, pb[q][c:], 0.0), jnp.concatenate([bd(mg1[q][:, :width]), bd(mg1[q][:, width:])], axis=1))
          for q in chunk_ids]
    drk_v = [_dot(jnp.where(incl, pk[q][c:], 0.0), v_bd[q]) for q in chunk_ids]
    bm = [_dot(b_end_T[:, qs[q]], mg1[q]) for q in chunk_ids]
    kv_end = [_dot(k_end_T[:, qs[q]], v[qs[q]]) for q in chunk_ids]
    maps = []
    for q in chunk_ids:
        m2 = r_t[qs[q]] + dm[q][:, :width]
        g2 = dm[q][:, width:] + drk_v[q]
        m3 = jnp.where(on_diag_block, bm[q][:, :width], 0.0) + jnp.where(eye_w, w_end[q * c:q * c + 1, :], 0.0)
        g3 = jnp.where(on_diag_block, bm[q][:, width:] + kv_end[q], 0.0)
        maps.append((m2, g2, m3, g3))

    st = st_ref[...]
    ys = []
    for m2, g2, m3, g3 in maps:
        ys.append(_dot(m2, st) + g2)
        st = _dot(m3, st) + g3
    st_ref[...] = st
    y = jnp.concatenate(ys, axis=0)
    inv_n = 1.0 / n
    mean = _dot_split_lhs(y, ones_h, terms=2) * inv_n
    var = _dot_split_lhs(jnp.square(y - mean), ones_h, terms=2) * inv_n
    yn = (y - mean) * lax.rsqrt(var + RWKV_LN_EPS)
    o_ref[0] = ((yn * ln_w + ln_b) + bonus) * gate


def _rwkv(z_rw, mu, vecs, wa2, g2, batch, seq):
    tb = min(RWKV_TB, seq)
    width = RWKV_HEADS * RWKV_HEAD_DIM
    head = np.arange(width) // RWKV_HEAD_DIM
    ones_h = jnp.asarray((head[:, None] == head[None, :]).astype(np.float32))
    z3 = z_rw.reshape(batch, seq, z_rw.shape[-1])
    return pl.pallas_call(
        _rwkv_kernel,
        grid=(batch, seq // tb),
        in_specs=[pl.BlockSpec((1, tb, z3.shape[-1]), lambda b, i: (b, i, 0))]
        + [_spec(w) for w in (mu, vecs, wa2, g2, ones_h)],
        out_specs=pl.BlockSpec((1, tb, width), lambda b, i: (b, i, 0)),
        out_shape=jax.ShapeDtypeStruct((batch, seq, width), F32),
        scratch_shapes=[pltpu.VMEM((1, z3.shape[-1]), F32), pltpu.VMEM((width, width), F32)],
        compiler_params=_cparams("parallel", "arbitrary"),
    )(z3, *map(_arr, (mu, vecs, wa2, g2, ones_h)))


S5_TB = 512


def _s5_kernel(u_ref, perm_ref, win_ref, abar_ref, pow_ref, cout_ref, d_ref, wglu_ref, o_ref, xr_ref, xi_ref, carry_ref):
    tb = u_ref.shape[1]
    ns = S5_GROUPS * S5_STATE
    steps = tb // SUBLANES

    @pl.when(pl.program_id(1) == 0)
    def _():
        carry_ref[...] = jnp.zeros_like(carry_ref)

    u = u_ref[0]
    u_perm = jnp.dot(perm_ref[0], u.astype(MXU_DTYPE), preferred_element_type=F32)
    bu = _dot(u_perm, win_ref[...])
    xr_ref[...] = bu[:, :ns]
    xi_ref[...] = bu[:, ns:]
    ar, ai = abar_ref[0:1, :], abar_ref[1:2, :]

    def scan_step(s, state):
        sr, si = state
        rows = pl.ds(pl.multiple_of(s * SUBLANES, SUBLANES), SUBLANES)
        nr = ar * sr - ai * si + xr_ref[rows, :]
        ni = ar * si + ai * sr + xi_ref[rows, :]
        xr_ref[rows, :] = nr
        xi_ref[rows, :] = ni
        return nr, ni

    zero = jnp.zeros((SUBLANES, ns), F32)
    fr, fi = lax.fori_loop(0, steps, scan_step, (zero, zero), unroll=4)

    cr, ci = abar_ref[2:3, :], abar_ref[3:4, :]
    er, ei = [carry_ref[0:1, :]], [carry_ref[1:2, :]]
    for c in range(SUBLANES):
        pr, pi = er[-1], ei[-1]
        er.append(fr[c:c + 1, :] + (cr * pr - ci * pi))
        ei.append(fi[c:c + 1, :] + (cr * pi + ci * pr))
    carry_ref[0:1, :] = er[SUBLANES]
    carry_ref[1:2, :] = ei[SUBLANES]
    enter_r = jnp.concatenate(er[:SUBLANES], axis=0)
    enter_i = jnp.concatenate(ei[:SUBLANES], axis=0)

    def add_entering(s, _):
        rows = pl.ds(pl.multiple_of(s * SUBLANES, SUBLANES), SUBLANES)
        pr, pi = pow_ref[0, pl.ds(s, 1), :], pow_ref[1, pl.ds(s, 1), :]
        xr_ref[rows, :] = xr_ref[rows, :] + (pr * enter_r - pi * enter_i)
        xi_ref[rows, :] = xi_ref[rows, :] + (pr * enter_i + pi * enter_r)
        return 0

    lax.fori_loop(0, steps, add_entering, 0, unroll=4)
    y_perm = _dot(xr_ref[...], cout_ref[0]) + _dot(xi_ref[...], cout_ref[1])
    y = _gelu(_dot_split_rhs(perm_ref[1], y_perm, terms=2) + d_ref[...] * u)
    o_ref[0] = y * _sigmoid(_dot(y, wglu_ref[...]))


def _s5_weights(lam_re, lam_im, log_dt, b_re, b_im, c_re, c_im, d, w_glu):
    g, p, hw = S5_GROUPS, S5_STATE, S5_GROUP_WIDTH
    dt = jnp.exp(log_dt.astype(F32))[:, None]
    lr, li = lam_re.astype(F32), lam_im.astype(F32)
    mag = jnp.exp(lr * dt)
    ar, ai = mag * jnp.cos(li * dt), mag * jnp.sin(li * dt)
    den = lr * lr + li * li
    cr = ((ar - 1.0) * lr + ai * li) / den
    ci = (ai * lr - (ar - 1.0) * li) / den
    eye = jnp.eye(g, dtype=F32)
    w_re = cr[:, :, None] * b_re - ci[:, :, None] * b_im
    w_im = cr[:, :, None] * b_im + ci[:, :, None] * b_re
    bd_in = lambda w: jnp.einsum("gph,ga->ghap", w, eye).reshape(g * hw, g * p)
    win = jnp.concatenate([bd_in(w_re), bd_in(w_im)], axis=1).astype(MXU_DTYPE)
    bd_out = lambda w: jnp.einsum("ghp,ga->gpah", w, eye).reshape(g * p, g * hw)
    cout = jnp.stack([bd_out(c_re.astype(F32)), -bd_out(c_im.astype(F32))]).astype(MXU_DTYPE)
    pr, pi = ar.reshape(1, g * p), ai.reshape(1, g * p)
    steps = S5_TB // SUBLANES
    while pr.shape[0] < steps:
        tr, ti = pr[-1:], pi[-1:]
        pr, pi = (jnp.concatenate([pr, pr * tr - pi * ti], axis=0), jnp.concatenate([pi, pr * ti + pi * tr], axis=0))
    powers = jnp.stack([pr[:steps], pi[:steps]])
    abar = jnp.concatenate([pr[0:1], pi[0:1], pr[steps - 1:steps], pi[steps - 1:steps]], axis=0)
    return win, abar, powers, cout, d.reshape(1, g * hw).astype(F32), w_glu.astype(MXU_DTYPE)


def _s5(u, weights, batch, seq):
    win, abar, powers, cout, d, wglu = weights
    tb = S5_TB
    assert seq % tb == 0
    ns = S5_GROUPS * S5_STATE
    width = S5_GROUPS * S5_GROUP_WIDTH
    steps = tb // SUBLANES
    src = (np.arange(tb) % SUBLANES) * steps + np.arange(tb) // SUBLANES
    fwd = (src[:, None] == np.arange(tb)[None, :]).astype(np.float32)
    perm = jnp.asarray(np.stack([fwd, fwd.T]), dtype=MXU_DTYPE)
    u3 = u.reshape(batch, seq, width)
    blk = pl.BlockSpec((1, tb, width), lambda b, i: (b, i, 0))
    return pl.pallas_call(
        _s5_kernel,
        grid=(batch, seq // tb),
        in_specs=[blk] + [_spec(w) for w in (perm, win, abar, powers, cout, d, wglu)],
        out_specs=blk,
        out_shape=jax.ShapeDtypeStruct((batch, seq, width), F32),
        scratch_shapes=[pltpu.VMEM((tb, ns), F32), pltpu.VMEM((tb, ns), F32), pltpu.VMEM((SUBLANES, ns), F32)],
        compiler_params=_cparams("parallel", "arbitrary"),
    )(u3, *map(_arr, (perm, win, abar, powers, cout, d, wglu)))


def _merge_kernel(x_ref, yn_ref, yr_ref, ys_ref, mg_ref, wun_ref, wur_ref, wus_ref, wout_ref, g_ref, o_ref):
    d = x_ref.shape[1]
    merged = (mg_ref[:, 0:d] * _dot(yn_ref[...], wun_ref[...]) + mg_ref[:, d:2 * d] * _dot(yr_ref[...], wur_ref[...])
              + mg_ref[:, 2 * d:3 * d] * _dot(ys_ref[...], wus_ref[...]))
    o_ref[...] = x_ref[...] + _rms(_dot(merged, wout_ref[...]), g_ref[...])


def _merge(x2, y_nsa, y_rwkv, y_s5, mg, wun, wur, wus, wout, gain, tm):
    m, d = x2.shape
    row = lambda a: pl.BlockSpec((tm, a.shape[1]), lambda i: (i, 0))
    return pl.pallas_call(
        _merge_kernel,
        grid=(m // tm,),
        in_specs=[row(x2), row(y_nsa), row(y_rwkv), row(y_s5), row(mg)]
        + [_spec(w) for w in (wun, wur, wus, wout, gain)],
        out_specs=row(x2),
        out_shape=jax.ShapeDtypeStruct((m, d), F32),
        compiler_params=_cparams("parallel"),
    )(x2, y_nsa, y_rwkv, y_s5, mg, *map(_arr, (wun, wur, wus, wout, gain)))


def _memkv_kernel(m_ref, g_ref, w_ref, o_ref):
    o_ref[0] = _dot(_rms(m_ref[0], g_ref[...]), w_ref[...]).astype(o_ref.dtype)


def _memkv(mem, gain, wkv):
    b, n, d = mem.shape
    depth, _, w2 = wkv.shape
    return pl.pallas_call(
        _memkv_kernel,
        grid=(depth, b),
        in_specs=[pl.BlockSpec((1, n, d), lambda l, i: (i, 0, 0)), pl.BlockSpec((None, 1, d), lambda l, i: (l, 0, 0)),
                  pl.BlockSpec((None, d, w2), lambda l, i: (l, 0, 0))],
        out_specs=pl.BlockSpec((None, 1, n, w2), lambda l, i: (l, i, 0, 0)),
        out_shape=jax.ShapeDtypeStruct((depth, b, n, w2), MXU_DTYPE),
        compiler_params=_cparams("parallel", "parallel"),
    )(mem, gain, wkv)


def _xattn_kernel(x_ref, kv_ref, gin_ref, gout_ref, wq_ref, wo_ref, o_ref):
    x = x_ref[0]
    width = XA_HEADS * XA_HEAD_DIM
    q = (_dot(_rms(x, gin_ref[...]), wq_ref[...]) * (XA_HEAD_DIM ** -0.5)).astype(MXU_DTYPE)
    k = kv_ref[0, :, 0:width]
    v = kv_ref[0, :, width:2 * width]
    head = lax.broadcasted_iota(jnp.int32, q.shape, 1) // XA_HEAD_DIM
    att = jnp.zeros(q.shape, F32)
    for h in range(XA_HEADS):
        mine = head == h
        p = _softmax_last(_dot_nt(jnp.where(mine, q, jnp.zeros_like(q)), k))
        att = att + jnp.where(mine, _dot(p, v), 0.0)
    o_ref[0] = x + _rms(_dot(att, wo_ref[...]), gout_ref[...])


def _xattn(x3, kv, gin, gout, wq, wo, tm):
    b, s, d = x3.shape
    blk = pl.BlockSpec((1, tm, d), lambda i, j: (i, j, 0))
    return pl.pallas_call(
        _xattn_kernel,
        grid=(b, s // tm),
        in_specs=[blk, pl.BlockSpec((None, 1) + tuple(kv.shape[1:]), lambda i, j: tuple(kv.idx) + (i, 0, 0))]
        + [_spec(w) for w in (gin, gout, wq, wo)],
        out_specs=blk,
        out_shape=jax.ShapeDtypeStruct(x3.shape, F32),
        compiler_params=_cparams("parallel", "parallel"),
    )(x3, *map(_arr, (kv, gin, gout, wq, wo)))


FFN_CHUNK = 256


def _ffn_kernel(x_ref, gin_ref, gout_ref, wg_ref, wu_ref, wd_ref, o_ref):
    x = x_ref[...]
    h = _rms(x, gin_ref[...]).astype(MXU_DTYPE)
    acc = jnp.zeros(x.shape, F32)
    for c in range(wg_ref.shape[1] // FFN_CHUNK):
        sl = slice(c * FFN_CHUNK, (c + 1) * FFN_CHUNK)
        a = jnp.dot(h, wg_ref[:, sl], preferred_element_type=F32)
        b = jnp.dot(h, wu_ref[:, sl], preferred_element_type=F32)
        acc = acc + _dot(a * _sigmoid(a) * b, wd_ref[sl, :])
    o_ref[...] = x + _rms(acc, gout_ref[...])


def _ffn(x2, gin, gout, wg, wu, wd, tm):
    m, d = x2.shape
    assert wg.shape[1] % FFN_CHUNK == 0
    row = pl.BlockSpec((tm, d), lambda i: (i, 0))
    return pl.pallas_call(
        _ffn_kernel,
        grid=(m // tm,),
        in_specs=[row] + [_spec(w) for w in (gin, gout, wg, wu, wd)],
        out_specs=row,
        out_shape=jax.ShapeDtypeStruct((m, d), F32),
        compiler_params=_cparams("parallel"),
    )(x2, *map(_arr, (gin, gout, wg, wu, wd)))


ROW_TILE = 512
INPROJ_ROW_TILE = 256


def kernel(x, mem, norm_gains, mem_norm, w_in, nsa_cmp_pos_k, nsa_cmp_pos_v, nsa_ck_w1, nsa_ck_w2, nsa_cv_w1, nsa_cv_w2, rwkv_mu, rwkv_w0, rwkv_w2, rwkv_a0, rwkv_a2, rwkv_g2, rwkv_k_k, rwkv_k_a, rwkv_r_k, rwkv_ln_w, rwkv_ln_b, s5_lam_re, s5_lam_im, s5_log_dt, s5_b_re, s5_b_im, s5_c_re, s5_c_im, s5_d, s5_w_glu, w_up_nsa, w_up_rwkv, w_up_s5, w_out, xa_w_q, xa_w_k, xa_w_v, xa_w_o, ffn_w_gate, ffn_w_up, ffn_w_down):
    batch, seq, d = x.shape
    depth = w_in.shape[0]
    tm = min(ROW_TILE, seq)
    bf = lambda a: a.astype(MXU_DTYPE)

    half = NSA_HEAD_DIM // 2
    inv = 1.0 / (ROPE_THETA ** (jnp.arange(0, NSA_HEAD_DIM, 2, dtype=F32) / NSA_HEAD_DIM))
    ang = jnp.arange(seq, dtype=F32)[:, None] * inv[None, :]
    cos, sin = jnp.cos(ang), jnp.sin(ang)
    cos128 = jnp.tile(cos, (1, LANES // half))
    sin128 = jnp.tile(jnp.concatenate([-sin, sin], axis=1), (1, LANES // NSA_HEAD_DIM))

    gains = norm_gains.reshape(depth, norm_gains.shape[1], 1, d)
    w_all = jax.vmap(_regroup_w_in)(w_in)
    cmp_k = jax.vmap(_compress_weights)(nsa_cmp_pos_k, nsa_ck_w1, nsa_ck_w2)
    cmp_v = jax.vmap(_compress_weights)(nsa_cmp_pos_v, nsa_cv_w1, nsa_cv_w2)
    width = RWKV_HEADS * RWKV_HEAD_DIM
    zero = jnp.zeros((depth, rwkv_w2.shape[1], width), F32)
    wa2 = bf(jnp.stack([jnp.concatenate([rwkv_w2, zero], axis=1), jnp.concatenate([zero, rwkv_a2], axis=1)], axis=1))
    vecs = jnp.stack([rwkv_w0, rwkv_a0, rwkv_k_k, rwkv_k_a, rwkv_r_k.reshape(depth, width), rwkv_ln_w, rwkv_ln_b,
                      jnp.zeros((depth, width), F32)], axis=1)
    mu = rwkv_mu.reshape(depth, 1, -1)
    g2 = bf(rwkv_g2)
    s5w = jax.vmap(_s5_weights)(s5_lam_re, s5_lam_im, s5_log_dt, s5_b_re, s5_b_im, s5_c_re, s5_c_im, s5_d, s5_w_glu)
    wun, wur, wus, wout = bf(w_up_nsa), bf(w_up_rwkv), bf(w_up_s5), bf(w_out)
    wq, wo = bf(xa_w_q), bf(xa_w_o)
    wg, wu, wd = bf(ffn_w_gate), bf(ffn_w_up), bf(ffn_w_down)
    kv_all = _memkv(mem, mem_norm.reshape(depth, 1, d), bf(jnp.concatenate([xa_w_k, xa_w_v], axis=-1)))

    x2 = x.reshape(batch * seq, d)
    for l in range(depth):
        of = lambda a: _Of(a, l)
        gain = lambda i: _Of(gains, l, i)
        z = _inproj(x2, gain(0), cos128, sin128, of(w_all), seq, min(INPROJ_ROW_TILE, seq))

        chunks = lambda a: a.reshape(batch, seq // NSA_CMP_STRIDE, NSA_CMP_STRIDE * a.shape[-1])
        kc2, vc2 = _compress(chunks(z["kc"]), chunks(z["vc"]), *map(of, cmp_k), *map(of, cmp_v))
        y_nsa = _nsa(z, kc2, vc2, batch, seq)
        y_rwkv = _rwkv(z["rw"], of(mu), of(vecs), of(wa2), of(g2), batch, seq)
        y_s5 = _s5(z["s5"], tuple(map(of, s5w)), batch, seq)

        flat = lambda a: a.reshape(batch * seq, a.shape[-1])
        x2 = _merge(x2, flat(y_nsa), flat(y_rwkv), flat(y_s5), z["mg"], of(wun), of(wur), of(wus), of(wout), gain(1), tm)
        x2 = _xattn(x2.reshape(batch, seq, d), of(kv_all), gain(2), gain(3), of(wq), of(wo), tm).reshape(batch * seq, d)
        x2 = _ffn(x2, gain(4), gain(5), of(wg), of(wu), of(wd), tm)
    return x2.reshape(batch, seq, d)
```

```python
import functools
import math

import jax
import jax.numpy as jnp
import numpy as np
from jax import lax
from jax.experimental import pallas as pl
from jax.experimental.pallas import tpu as pltpu

NSA_HEADS = 8
NSA_KV_GROUPS = 2
NSA_HEAD_DIM = 64
NSA_CMP_LEN = 32
NSA_CMP_STRIDE = 16
NSA_SEL_LEN = 64
NSA_SEL_TOPK = 16
NSA_WINDOW = 512
RWKV_HEADS = 4
RWKV_HEAD_DIM = 64
RWKV_LN_EPS = 64e-5
S5_GROUPS = 16
S5_GROUP_WIDTH = 16
S5_STATE = 64
XA_HEADS = 4
XA_HEAD_DIM = 64
ROPE_THETA = 10000.0
NORM_EPS = 1e-6
NEG_INF = -1e30
SEL_FORCE_SCORE = 1e9

LANES = 128
SUBLANES = 8
MXU_DTYPE = jnp.bfloat16
VMEM_LIMIT = 56 << 20

F32 = jnp.float32


def _cparams(*sem):
    return pltpu.CompilerParams(dimension_semantics=sem, vmem_limit_bytes=VMEM_LIMIT)


class _Of:
    def __init__(self, arr, *idx):
        self.arr, self.idx = arr, idx

    @property
    def shape(self):
        return self.arr.shape[len(self.idx):]


def _arr(w):
    return w.arr if isinstance(w, _Of) else w


def _spec(w):
    nd = len(w.shape)
    if isinstance(w, _Of):
        lead = tuple(w.idx)
        return pl.BlockSpec((None,) * len(lead) + tuple(w.shape), lambda *_: lead + (0,) * nd,
                            pipeline_mode=pl.Buffered(1))
    return pl.BlockSpec(w.shape, lambda *_: (0,) * nd, pipeline_mode=pl.Buffered(1))


def _dot(a, b):
    return jnp.dot(a.astype(MXU_DTYPE), b.astype(MXU_DTYPE), preferred_element_type=F32)


def _dot_nt(a, b):
    return lax.dot_general(a.astype(MXU_DTYPE), b.astype(MXU_DTYPE), (((1,), (1,)), ((), ())),
                           preferred_element_type=F32)


def _split(a, terms):
    parts = []
    for _ in range(terms - 1):
        p = a.astype(MXU_DTYPE)
        parts.append(p)
        a = a - p.astype(F32)
    parts.append(a.astype(MXU_DTYPE))
    return parts


def _dot_split_lhs(a, b01, terms=3):
    b01 = b01.astype(MXU_DTYPE)
    return sum(jnp.dot(p, b01, preferred_element_type=F32) for p in _split(a, terms))


def _dot_split_rhs(a01, b, terms=3):
    a01 = a01.astype(MXU_DTYPE)
    return sum(jnp.dot(a01, p, preferred_element_type=F32) for p in _split(b, terms))


def _rms(x, g):
    return x * lax.rsqrt(jnp.mean(x * x, axis=-1, keepdims=True) + NORM_EPS) * g


def _sigmoid(x):
    return 1.0 / (1.0 + jnp.exp(-x))


def _gelu(x):
    return 0.5 * x * (1.0 + jnp.tanh(math.sqrt(2.0 / math.pi) * (x + 0.044715 * (x * x * x))))


def _softmax_last(s):
    m = jnp.max(s, axis=-1, keepdims=True)
    e = jnp.exp(s - m)
    return e / jnp.sum(e, axis=-1, keepdims=True)


_SEG = dict(q=(0, 512), kc=(512, 640), vc=(640, 768), ks=(768, 1024), vs=(1024, 1280), kw=(1280, 1536),
            vw=(1536, 1792), ng=(1792, 1920), rw=(1920, 2944), s5=(2944, 3200), mg=(3200, 6272))
_W_IN_COLS = 6272


def _rope128(x, cos, sin_signed):
    lane = lax.broadcasted_iota(jnp.int32, x.shape, 1)
    first = (lane % NSA_HEAD_DIM) < (NSA_HEAD_DIM // 2)
    rot = jnp.where(first, pltpu.roll(x, LANES - NSA_HEAD_DIM // 2, 1), pltpu.roll(x, NSA_HEAD_DIM // 2, 1))
    return x * cos + rot * sin_signed


def _inproj_kernel(x_ref, g_ref, cos_ref, sin_ref, w_ref,
                   qc_o, qr_o, kc_o, vc_o, ks_o, vs_o, kw_o, vw_o, ng_o, rw_o, s5_o, mg_o, *, n_tab):
    h = _rms(x_ref[...], g_ref[...]).astype(MXU_DTYPE)
    cos = cos_ref[...]
    sin = sin_ref[...]
    scale = NSA_HEAD_DIM ** -0.5 * math.log2(math.e)

    def seg(name):
        a, b = _SEG[name]
        return jnp.dot(h, w_ref[:, a:b], preferred_element_type=F32)

    q = seg("q")
    qc_o[...] = (q * scale).astype(qc_o.dtype)
    for c in range(q.shape[1] // LANES):
        sl = slice(c * LANES, (c + 1) * LANES)
        qr_o[:, sl] = (_rope128(q[:, sl], cos, sin) * scale).astype(qr_o.dtype)
    kc_o[...] = seg("kc")
    vc_o[...] = seg("vc")
    k = seg("kw")
    for c in range(NSA_KV_GROUPS):
        sl = slice(c * LANES, (c + 1) * LANES)
        kw_o[:, sl] = _rope128(k[:, sl], cos, sin).astype(kw_o.dtype)
    tm = x_ref.shape[0]
    tok = (pl.program_id(0) % n_tab) * tm + lax.broadcasted_iota(jnp.int32, (tm, LANES), 0)
    onehot = jnp.where(lax.broadcasted_iota(jnp.int32, (tm, LANES), 1) == tok // NSA_SEL_LEN, 1.0, 0.0)
    k = seg("ks")
    for c in range(NSA_KV_GROUPS):
        ks_o[:, 2 * c * LANES:(2 * c + 1) * LANES] = _rope128(k[:, c * LANES:(c + 1) * LANES], cos, sin).astype(ks_o.dtype)
        ks_o[:, (2 * c + 1) * LANES:(2 * c + 2) * LANES] = onehot.astype(ks_o.dtype)
    ones = jnp.ones((tm, LANES), vs_o.dtype)
    for name, out in (("vs", vs_o), ("vw", vw_o)):
        val = seg(name)
        for c in range(NSA_KV_GROUPS):
            out[:, 2 * c * LANES:(2 * c + 1) * LANES] = val[:, c * LANES:(c + 1) * LANES].astype(out.dtype)
            out[:, (2 * c + 1) * LANES:(2 * c + 2) * LANES] = ones
    ng_o[...] = _sigmoid(seg("ng"))
    rw_o[...] = seg("rw")
    s5_o[...] = seg("s5")
    mg_o[...] = _sigmoid(seg("mg")).astype(mg_o.dtype)


def _dup_groups(w):
    hd = NSA_HEAD_DIM
    return jnp.concatenate([w[:, :hd], w[:, :hd], w[:, hd:], w[:, hd:]], axis=1)


def _regroup_w_in(w):
    d = w.shape[0]
    q_w = NSA_HEADS * NSA_HEAD_DIM
    kv_w = NSA_KV_GROUPS * NSA_HEAD_DIM
    o = 0
    wq = w[:, o:o + q_w]; o += q_w
    kv = [w[:, o + i * kv_w:o + (i + 1) * kv_w] for i in range(6)]; o += 6 * kv_w
    wng = w[:, o:o + 3 * NSA_HEADS]; o += 3 * NSA_HEADS
    rw_w = 3 * RWKV_HEADS * RWKV_HEAD_DIM + 256
    wrw = w[:, o:o + rw_w]; o += rw_w
    s5_w = S5_GROUPS * S5_GROUP_WIDTH
    ws5 = w[:, o:o + s5_w]; o += s5_w
    wmg = w[:, o:]
    wng = jnp.pad(wng, ((0, 0), (0, LANES - wng.shape[1])))
    out = jnp.concatenate([wq, kv[0], kv[1], _dup_groups(kv[2]), _dup_groups(kv[3]), _dup_groups(kv[4]),
                           _dup_groups(kv[5]), wng, wrw, ws5, wmg], axis=1)
    assert out.shape == (d, _W_IN_COLS), out.shape
    return out.astype(MXU_DTYPE)


def _inproj(x2, gain, cos128, sin128, w_all, seq, tm):
    m, d = x2.shape
    n_tab = seq // tm
    row = lambda w: pl.BlockSpec((tm, w), lambda i: (i, 0))
    tab = pl.BlockSpec((tm, LANES), lambda i: (i % n_tab, 0))
    widths = dict(qc=512, qr=512, kc=128, vc=128, ks=512, vs=512, kw=256, vw=512, ng=128, rw=1024, s5=256, mg=3072)
    dtypes = dict(qc=MXU_DTYPE, qr=MXU_DTYPE, kc=F32, vc=F32, ks=MXU_DTYPE, vs=MXU_DTYPE, kw=MXU_DTYPE,
                  vw=MXU_DTYPE, ng=F32, rw=F32, s5=F32, mg=MXU_DTYPE)
    names = list(widths)
    outs = pl.pallas_call(
        functools.partial(_inproj_kernel, n_tab=n_tab),
        grid=(m // tm,),
        in_specs=[row(d), _spec(gain), tab, tab, _spec(w_all)],
        out_specs=[row(widths[n]) for n in names],
        out_shape=[jax.ShapeDtypeStruct((m, widths[n]), dtypes[n]) for n in names],
        compiler_params=_cparams("parallel"),
    )(x2, _arr(gain), cos128, sin128, _arr(w_all))
    return dict(zip(names, outs))


def _compress_kernel(k_ref, v_ref, pk_ref, pv_ref, kw1_ref, kw2_ref, vw1_ref, vw2_ref, kc_o, vc_o):
    def one(x_ref, p_ref, w1_ref, w2_ref):
        x = x_ref[0]
        n = x.shape[0]
        top = _dot(x + p_ref[0:1, :], w1_ref[0])
        bot = _dot(x + p_ref[1:2, :], w1_ref[1])
        hid = top + pltpu.roll(bot, n - 1, 0)
        return _dot(_gelu(hid), w2_ref[...])

    kc_o[0] = one(k_ref, pk_ref, kw1_ref, kw2_ref).astype(kc_o.dtype)
    vc = one(v_ref, pv_ref, vw1_ref, vw2_ref)
    for c in range(NSA_KV_GROUPS):
        vc_o[0, :, 2 * c * LANES:(2 * c + 1) * LANES] = vc[:, c * LANES:(c + 1) * LANES].astype(vc_o.dtype)
        vc_o[0, :, (2 * c + 1) * LANES:(2 * c + 2) * LANES] = jnp.ones((vc.shape[0], LANES), vc_o.dtype)


def _compress_weights(pos, w1, w2):
    g, hd, half = NSA_KV_GROUPS, NSA_HEAD_DIM, NSA_CMP_STRIDE
    hidden = w1.shape[1]
    eye = jnp.eye(g, dtype=w1.dtype)
    w1r = w1.reshape(2, half, hd, hidden)
    w1e = jnp.einsum("pldj,ab->pladbj", w1r, eye).reshape(2, half * g * hd, g * hidden)
    w2d = jnp.concatenate([w2, w2], axis=1)
    w2e = jnp.einsum("jd,ab->ajbd", w2d, eye).reshape(g * hidden, g * 2 * hd)
    pos_e = jnp.broadcast_to(pos.reshape(2, half, 1, hd), (2, half, g, hd)).reshape(2, half * g * hd)
    return pos_e.astype(F32), w1e.astype(MXU_DTYPE), w2e.astype(MXU_DTYPE)


def _compress(kc, vc, pk, kw1, kw2, pv, vw1, vw2):
    b, n, width = kc.shape
    blk = pl.BlockSpec((1, n, width), lambda i: (i, 0, 0))
    out_w = kw2.shape[1]
    out_blk = lambda w: pl.BlockSpec((1, n, w), lambda i: (i, 0, 0))
    return pl.pallas_call(
        _compress_kernel,
        grid=(b,),
        in_specs=[blk, blk] + [_spec(w) for w in (pk, pv, kw1, kw2, vw1, vw2)],
        out_specs=[out_blk(out_w), out_blk(2 * out_w)],
        out_shape=[jax.ShapeDtypeStruct((b, n, out_w), MXU_DTYPE), jax.ShapeDtypeStruct((b, n, 2 * out_w), MXU_DTYPE)],
        compiler_params=_cparams("parallel"),
    )(kc, vc, *map(_arr, (pk, pv, kw1, kw2, vw1, vw2)))


NSA_TQ = 128
NSA_TK = 1024
MASKED = -1e30


def _nsa_kernel(qc_ref, qr_ref, kc_ref, vc_ref, ks_ref, vs_ref, kw_ref, vw_ref, ng_ref, ovl_ref, gexp_ref, y_ref, *, top_k):
    tq = qc_ref.shape[1]
    seq = ks_ref.shape[1]
    n_cmp_pad = kc_ref.shape[1]
    hpg = NSA_HEADS // NSA_KV_GROUPS
    rows = hpg * tq
    n_sel = seq // NSA_SEL_LEN
    n_sel_pad = -(-n_sel // SUBLANES) * SUBLANES
    t0 = pl.program_id(1) * tq
    tcol = t0 + lax.broadcasted_iota(jnp.int32, (tq, 1), 0)
    low_half = lax.broadcasted_iota(jnp.int32, (tq, LANES), 1) < NSA_HEAD_DIM

    def stack_heads(q_ref, g):
        parts = []
        for r in range(hpg):
            h = g * hpg + r
            slab = q_ref[0, :, (h // 2) * LANES:(h // 2 + 1) * LANES]
            keep = low_half if h % 2 == 0 else jnp.logical_not(low_half)
            parts.append(jnp.where(keep, slab, jnp.zeros_like(slab)))
        return jnp.concatenate(parts, axis=0)

    def unstack(o, g, y_parts):
        for pair in range(hpg // 2):
            even = o[(2 * pair) * tq:(2 * pair + 1) * tq]
            odd = o[(2 * pair + 1) * tq:(2 * pair + 2) * tq]
            y_parts[g * (hpg // 2) + pair] = jnp.where(low_half, even, odd)

    def masked_exp_pv(q, k, bias, v_ones):
        s = [_dot_nt(q[g], k[g]) for g in groups]
        n = s[0].shape[-1]
        s = [sg.reshape(hpg, tq, n) + bias[None] for sg in s]
        e = [jnp.exp2(sg - jnp.max(sg, axis=-1, keepdims=True)) for sg in s]
        return e, [_dot(e[g].reshape(rows, n), v_ones[g]) for g in groups]

    cmp_end = lax.broadcasted_iota(jnp.int32, (1, n_cmp_pad), 1) * NSA_CMP_STRIDE + (NSA_CMP_LEN - 1)
    cmp_bias = jnp.where(cmp_end <= tcol, 0.0, NEG_INF)
    cmp_live = jnp.where(tcol >= NSA_CMP_LEN - 1, 1.0, 0.0)
    slab = NSA_WINDOW + tq
    w0 = pl.multiple_of(jnp.maximum(t0 - NSA_WINDOW, 0), tq)
    wpos = w0 + lax.broadcasted_iota(jnp.int32, (1, slab), 1)
    win_bias = jnp.where(wpos <= tcol, jnp.where(wpos > tcol - NSA_WINDOW, 0.0, NEG_INF), NEG_INF)
    n_kt = (t0 + tq - 1) // NSA_TK + 1
    k_last = pl.multiple_of((n_kt - 1) * NSA_TK, NSA_TK)
    diag_bias = jnp.where(k_last + lax.broadcasted_iota(jnp.int32, (1, NSA_TK), 1) <= tcol, 0.0, NEG_INF)

    blk = lax.broadcasted_iota(jnp.int32, (n_sel_pad, tq), 0)
    cur = (t0 + lax.broadcasted_iota(jnp.int32, (n_sel_pad, tq), 1)) // NSA_SEL_LEN
    forced = (blk == 0) | (blk == cur) | (blk == cur - 1)

    groups = range(NSA_KV_GROUPS)
    gl = [slice(g * LANES, (g + 1) * LANES) for g in groups]
    gx = [slice(2 * g * LANES, (2 * g + 2) * LANES) for g in groups]
    y_cmp, y_sel, y_win = [None] * 4, [None] * 4, [None] * 4
    qc = [stack_heads(qc_ref, g) for g in groups]
    qr = [stack_heads(qr_ref, g) for g in groups]

    cmp_e, cmp_pv = masked_exp_pv(qc, [kc_ref[0, :, gl[g]] for g in groups], cmp_bias, [vc_ref[0, :, gx[g]] for g in groups])
    cmp_inv = [(cmp_live[None] / cmp_pv[g][:, LANES:].reshape(hpg, tq, LANES)) for g in groups]
    for g in groups:
        unstack(cmp_pv[g][:, :LANES] * cmp_inv[g].reshape(rows, LANES), g, y_cmp)

    _, win = masked_exp_pv(qr, [kw_ref[0, pl.ds(w0, slab), gl[g]] for g in groups], win_bias,
                           [vw_ref[0, pl.ds(w0, slab), gx[g]] for g in groups])
    for g in groups:
        unstack(win[g][:, :LANES] / win[g][:, LANES:], g, y_win)

    qx = []
    for g in groups:
        p = cmp_e[g] * jnp.concatenate([cmp_inv[g]] * -(-n_cmp_pad // LANES), axis=-1)[..., :n_cmp_pad]
        imp = _dot_split_lhs(jnp.sum(p, axis=0), ovl_ref[...])
        vals = jnp.where(blk <= cur, jnp.where(forced, SEL_FORCE_SCORE, imp.T[:n_sel_pad]), NEG_INF)
        rank = [jnp.zeros((SUBLANES, tq), F32) for _ in range(n_sel_pad // SUBLANES)]
        for i in range(n_sel):
            other = vals[i:i + 1, :]
            for u in range(n_sel_pad // SUBLANES):
                mine = vals[u * SUBLANES:(u + 1) * SUBLANES]
                if i < u * SUBLANES:
                    ahead = other >= mine
                elif i >= (u + 1) * SUBLANES:
                    ahead = other > mine
                else:
                    ahead = (other > mine) | ((other == mine) & (blk[u * SUBLANES:(u + 1) * SUBLANES] > i))
                rank[u] = rank[u] + jnp.where(ahead, 1.0, 0.0)
        drop = jnp.where(jnp.concatenate(rank, axis=0) < top_k, 0.0, MASKED)
        drop = jnp.concatenate([drop, jnp.zeros((LANES - n_sel_pad, tq), F32)], axis=0).T.astype(MXU_DTYPE)
        qx.append(jnp.concatenate([qr[g], jnp.concatenate([drop] * hpg, axis=0)], axis=1))

    def sel_tiles(k0, bias, carries):
        s = [_dot_nt(qx[g], ks_ref[0, pl.ds(k0, NSA_TK), gx[g]]).reshape(hpg, tq, NSA_TK) for g in groups]
        if bias is not None:
            s = [sg + bias[None] for sg in s]
        m_new = [jnp.maximum(carries[g][0], jnp.max(s[g], axis=-1, keepdims=True)) for g in groups]
        e = [jnp.exp2(s[g] - m_new[g]) for g in groups]
        pv = [_dot(e[g].reshape(rows, NSA_TK), vs_ref[0, pl.ds(k0, NSA_TK), gx[g]]) for g in groups]
        return tuple((m_new[g], jnp.exp2(carries[g][0] - m_new[g]) * carries[g][1] + pv[g].reshape(hpg, tq, 2 * LANES))
                     for g in groups)

    init = (jnp.full((hpg, tq, 1), NEG_INF, F32), jnp.zeros((hpg, tq, 2 * LANES), F32))
    carries = lax.fori_loop(0, n_kt - 1, lambda j, c: sel_tiles(pl.multiple_of(j * NSA_TK, NSA_TK), None, c),
                            tuple(init for _ in groups))
    carries = sel_tiles(k_last, diag_bias, carries)
    for g in groups:
        acc = carries[g][1].reshape(rows, 2 * LANES)
        unstack(acc[:, :LANES] / acc[:, LANES:], g, y_sel)

    gates = ng_ref[0]
    q_w = NSA_HEADS * NSA_HEAD_DIM
    spread = _dot_split_lhs(gates, gexp_ref[...], terms=2)
    for c in range(4):
        sl = slice(c * LANES, (c + 1) * LANES)
        y_ref[0, :, sl] = (spread[:, sl] * y_cmp[c] + spread[:, q_w + c * LANES:q_w + (c + 1) * LANES] * y_sel[c]
                           + spread[:, 2 * q_w + c * LANES:2 * q_w + (c + 1) * LANES] * y_win[c]).astype(y_ref.dtype)


def _nsa_constants(seq, n_cmp_pad):
    n_sel = seq // NSA_SEL_LEN
    cmp_start = np.arange(n_cmp_pad) * NSA_CMP_STRIDE
    sel_start = np.arange(LANES) * NSA_SEL_LEN
    ovl = ((cmp_start[:, None] < sel_start[None, :] + NSA_SEL_LEN) & (cmp_start[:, None] + NSA_CMP_LEN > sel_start[None, :])
           & (np.arange(LANES)[None, :] < n_sel) & (np.arange(n_cmp_pad)[:, None] < n_cmp_pad - 1))
    q_w = NSA_HEADS * NSA_HEAD_DIM
    gexp = np.zeros((LANES, 3 * q_w), np.float32)
    for i in range(3):
        for h in range(NSA_HEADS):
            gexp[3 * h + i, i * q_w + h * NSA_HEAD_DIM:i * q_w + (h + 1) * NSA_HEAD_DIM] = 1.0
    return jnp.asarray(ovl.astype(np.float32), dtype=MXU_DTYPE), jnp.asarray(gexp, dtype=MXU_DTYPE)


def _nsa(z, kc2, vc2, batch, seq):
    tq = NSA_TQ
    q_w = NSA_HEADS * NSA_HEAD_DIM
    n_cmp_pad = kc2.shape[1]
    ovl, gexp = _nsa_constants(seq, n_cmp_pad)
    r3 = lambda a: a.reshape(batch, seq, a.shape[-1])
    qblk = pl.BlockSpec((1, tq, q_w), lambda b, i: (b, i, 0))
    full = lambda n, w: pl.BlockSpec((1, n, w), lambda b, i: (b, 0, 0))
    top_k = min(NSA_SEL_TOPK, seq // NSA_SEL_LEN)
    return pl.pallas_call(
        functools.partial(_nsa_kernel, top_k=top_k),
        grid=(batch, seq // tq),
        in_specs=[qblk, qblk, full(n_cmp_pad, 256), full(n_cmp_pad, 512), full(seq, 512), full(seq, 512),
                  full(seq, 256), full(seq, 512), pl.BlockSpec((1, tq, LANES), lambda b, i: (b, i, 0)),
                  _spec(ovl), _spec(gexp)],
        out_specs=qblk,
        out_shape=jax.ShapeDtypeStruct((batch, seq, q_w), MXU_DTYPE),
        compiler_params=_cparams("parallel", "arbitrary"),
    )(r3(z["qc"]), r3(z["qr"]), kc2, vc2, r3(z["ks"]), r3(z["vs"]), r3(z["kw"]), r3(z["vw"]), r3(z["ng"]),
      ovl, gexp)


RWKV_CHUNK = 64
RWKV_TB = 512


def _rwkv_kernel(z_ref, mu_ref, vecs_ref, wa2_ref, g2_ref, ones_ref, o_ref, prev_ref, st_ref):
    tb = z_ref.shape[1]
    width = RWKV_HEADS * RWKV_HEAD_DIM
    n = RWKV_HEAD_DIM
    c = RWKV_CHUNK

    @pl.when(pl.program_id(1) == 0)
    def _():
        prev_ref[...] = jnp.zeros_like(prev_ref)
        st_ref[...] = jnp.zeros_like(st_ref)

    z = z_ref[0]
    row = lax.broadcasted_iota(jnp.int32, (tb, 1), 0)
    z_prev = jnp.where(row == 0, prev_ref[...], pltpu.roll(z, 1, 0))
    prev_ref[...] = z[tb - 1:tb, :]
    zs = z + (z_prev - z) * mu_ref[...]

    w0, a0, k_k, k_a, r_k, ln_w, ln_b = (vecs_ref[i:i + 1, :] for i in range(7))
    r = zs[:, 0:width]
    k = zs[:, width:2 * width]
    v = zs[:, 2 * width:3 * width]
    lr = zs[:, 3 * width:3 * width + LANES]
    w = w0 + _dot(jnp.tanh(lr), wa2_ref[0])
    w = -(jnp.maximum(-w, 0.0) + jnp.log(1.0 + jnp.exp(-jnp.abs(w)))) - 0.5
    logd = -jnp.exp(w)
    a = _sigmoid(a0 + _dot(lr, wa2_ref[1]))
    gate = _dot(_sigmoid(zs[:, 3 * width + LANES:]), g2_ref[...])
    ones_h = ones_ref[...]
    kk = k * k_k
    kk = kk / jnp.maximum(jnp.sqrt(_dot_split_lhs(kk * kk, ones_h, terms=2)), 1e-12)
    k = k * (1.0 + (a - 1.0) * k_a)
    bonus = _dot_split_lhs(r * k * r_k, ones_h, terms=2) * v

    ri = lax.broadcasted_iota(jnp.int32, (tb, tb), 0)
    ci = lax.broadcasted_iota(jnp.int32, (tb, tb), 1)
    same = (ri // c) == (ci // c)
    cl = _dot_split_rhs(jnp.where(same & (ci <= ri), 1.0, 0.0), logd)
    cend = jnp.concatenate([jnp.broadcast_to(cl[q * c + c - 1:q * c + c, :], (c, width)) for q in range(tb // c)],
                           axis=0)
    e_inv = jnp.exp(-cl)
    e_end = jnp.exp(cend - cl)
    a_t = -kk * jnp.exp(cl - logd)
    r_t = r * jnp.exp(cl)
    b = kk * a
    b_t = b * e_inv
    k_t = k * e_inv
    b_end_T = (b * e_end).T
    k_end_T = (k * e_end).T
    w_end = jnp.exp(cend)

    head_of_row = lax.broadcasted_iota(jnp.int32, (width, width), 0) // n
    head_of_lane = lax.broadcasted_iota(jnp.int32, (width, width), 1) // n
    on_diag_block = head_of_row == head_of_lane
    eye_w = lax.broadcasted_iota(jnp.int32, (width, width), 0) == lax.broadcasted_iota(jnp.int32, (width, width), 1)
    step = lax.broadcasted_iota(jnp.int32, (c, width), 0)
    within = lax.broadcasted_iota(jnp.int32, (c, width), 1) % n
    strict = within < step
    incl = within <= step
    eye_c = jnp.where(within == step, 1.0, 0.0)

    def bd(m):
        m = m.astype(MXU_DTYPE)
        return jnp.where(on_diag_block, jnp.concatenate([m] * RWKV_HEADS, axis=0), jnp.zeros((), MXU_DTYPE))

    chunk_ids = range(tb // c)
    qs = [slice(q * c, (q + 1) * c) for q in chunk_ids]
    xs = [jnp.concatenate([a_t[s], r_t[s]], axis=0) for s in qs]
    pb = [_dot_nt(xs[q], bd(b_t[qs[q]])) for q in chunk_ids]
    pk = [_dot_nt(xs[q], bd(k_t[qs[q]])) for q in chunk_ids]
    n_low = [jnp.where(strict, p[:c], 0.0) for p in pb]
    t_inv = [eye_c + jnp.where((step // 2) == (within // 2), m, 0.0) for m in n_low]
    size = 2
    while size < c:
        off = ((step // (2 * size)) == (within // (2 * size))) & ((step // size) != (within // size))
        y = [_dot(jnp.where(off, m, 0.0), bd(t)) for m, t in zip(n_low, t_inv)]
        t_inv = [t + _dot(t, bd(yy)) for t, yy in zip(t_inv, y)]
        size *= 2
    v_bd = [bd(v[s]) for s in qs]
    nv = [_dot(jnp.where(strict, pk[q][:c], 0.0), v_bd[q]) for q in chunk_ids]
    mg1 = [_dot(t_inv[q], jnp.concatenate([bd(a_t[qs[q]]), bd(nv[q])], axis=1)) for q in chunk_ids]
    dm = [_dot(jnp.where(incl, pb[q][c:], 0.0), jnp.concatenate([bd(mg1[q][:, :width]), bd(mg1[q][:, width:])], axis=1))
          for q in chunk_ids]
    drk_v = [_dot(jnp.where(incl, pk[q][c:], 0.0), v_bd[q]) for q in chunk_ids]
    bm = [_dot(b_end_T[:, qs[q]], mg1[q]) for q in chunk_ids]
    kv_end = [_dot(k_end_T[:, qs[q]], v[qs[q]]) for q in chunk_ids]
    maps = []
    for q in chunk_ids:
        m2 = r_t[qs[q]] + dm[q][:, :width]
        g2 = dm[q][:, width:] + drk_v[q]
        m3 = jnp.where(on_diag_block, bm[q][:, :width], 0.0) + jnp.where(eye_w, w_end[q * c:q * c + 1, :], 0.0)
        g3 = jnp.where(on_diag_block, bm[q][:, width:] + kv_end[q], 0.0)
        maps.append((m2, g2, m3, g3))

    st = st_ref[...]
    ys = []
    for m2, g2, m3, g3 in maps:
        ys.append(_dot(m2, st) + g2)
        st = _dot(m3, st) + g3
    st_ref[...] = st
    y = jnp.concatenate(ys, axis=0)
    inv_n = 1.0 / n
    mean = _dot_split_lhs(y, ones_h, terms=2) * inv_n
    var = _dot_split_lhs(jnp.square(y - mean), ones_h, terms=2) * inv_n
    yn = (y - mean) * lax.rsqrt(var + RWKV_LN_EPS)
    o_ref[0] = (((yn * ln_w + ln_b) + bonus) * gate).astype(o_ref.dtype)


def _rwkv(z_rw, mu, vecs, wa2, g2, batch, seq):
    tb = min(RWKV_TB, seq)
    width = RWKV_HEADS * RWKV_HEAD_DIM
    head = np.arange(width) // RWKV_HEAD_DIM
    ones_h = jnp.asarray((head[:, None] == head[None, :]).astype(np.float32))
    z3 = z_rw.reshape(batch, seq, z_rw.shape[-1])
    return pl.pallas_call(
        _rwkv_kernel,
        grid=(batch, seq // tb),
        in_specs=[pl.BlockSpec((1, tb, z3.shape[-1]), lambda b, i: (b, i, 0))]
        + [_spec(w) for w in (mu, vecs, wa2, g2, ones_h)],
        out_specs=pl.BlockSpec((1, tb, width), lambda b, i: (b, i, 0)),
        out_shape=jax.ShapeDtypeStruct((batch, seq, width), MXU_DTYPE),
        scratch_shapes=[pltpu.VMEM((1, z3.shape[-1]), F32), pltpu.VMEM((width, width), F32)],
        compiler_params=_cparams("parallel", "arbitrary"),
    )(z3, *map(_arr, (mu, vecs, wa2, g2, ones_h)))


S5_TB = 512


def _s5_kernel(u_ref, perm_ref, win_ref, abar_ref, pow_ref, cout_ref, d_ref, wglu_ref, o_ref, xr_ref, xi_ref, carry_ref):
    tb = u_ref.shape[1]
    ns = S5_GROUPS * S5_STATE
    steps = tb // SUBLANES

    @pl.when(pl.program_id(1) == 0)
    def _():
        carry_ref[...] = jnp.zeros_like(carry_ref)

    u = u_ref[0]
    u_perm = jnp.dot(perm_ref[0], u.astype(MXU_DTYPE), preferred_element_type=F32)
    xr_ref[...] = _dot(u_perm, win_ref[:, :ns])
    xi_ref[...] = _dot(u_perm, win_ref[:, ns:])
    ar, ai = abar_ref[0:1, :], abar_ref[1:2, :]

    def scan_step(s, state):
        sr, si = state
        rows = pl.ds(pl.multiple_of(s * SUBLANES, SUBLANES), SUBLANES)
        nr = ar * sr - ai * si + xr_ref[rows, :]
        ni = ar * si + ai * sr + xi_ref[rows, :]
        xr_ref[rows, :] = nr
        xi_ref[rows, :] = ni
        return nr, ni

    zero = jnp.zeros((SUBLANES, ns), F32)
    fr, fi = lax.fori_loop(0, steps, scan_step, (zero, zero), unroll=4)

    cr, ci = abar_ref[2:3, :], abar_ref[3:4, :]
    er, ei = [carry_ref[0:1, :]], [carry_ref[1:2, :]]
    for c in range(SUBLANES):
        pr, pi = er[-1], ei[-1]
        er.append(fr[c:c + 1, :] + (cr * pr - ci * pi))
        ei.append(fi[c:c + 1, :] + (cr * pi + ci * pr))
    carry_ref[0:1, :] = er[SUBLANES]
    carry_ref[1:2, :] = ei[SUBLANES]
    enter_r = jnp.concatenate(er[:SUBLANES], axis=0)
    enter_i = jnp.concatenate(ei[:SUBLANES], axis=0)

    def add_entering(s, _):
        rows = pl.ds(pl.multiple_of(s * SUBLANES, SUBLANES), SUBLANES)
        pr, pi = pow_ref[0, pl.ds(s, 1), :], pow_ref[1, pl.ds(s, 1), :]
        xr_ref[rows, :] = xr_ref[rows, :] + (pr * enter_r - pi * enter_i)
        xi_ref[rows, :] = xi_ref[rows, :] + (pr * enter_i + pi * enter_r)
        return 0

    lax.fori_loop(0, steps, add_entering, 0, unroll=4)
    y_perm = _dot(xr_ref[...], cout_ref[0]) + _dot(xi_ref[...], cout_ref[1])
    y = _gelu(_dot_split_rhs(perm_ref[1], y_perm, terms=2) + d_ref[...] * u)
    o_ref[0] = (y * _sigmoid(_dot(y, wglu_ref[...]))).astype(o_ref.dtype)


def _s5_weights(lam_re, lam_im, log_dt, b_re, b_im, c_re, c_im, d, w_glu):
    g, p, hw = S5_GROUPS, S5_STATE, S5_GROUP_WIDTH
    dt = jnp.exp(log_dt.astype(F32))[:, None]
    lr, li = lam_re.astype(F32), lam_im.astype(F32)
    mag = jnp.exp(lr * dt)
    ar, ai = mag * jnp.cos(li * dt), mag * jnp.sin(li * dt)
    den = lr * lr + li * li
    cr = ((ar - 1.0) * lr + ai * li) / den
    ci = (ai * lr - (ar - 1.0) * li) / den
    eye = jnp.eye(g, dtype=F32)
    w_re = cr[:, :, None] * b_re - ci[:, :, None] * b_im
    w_im = cr[:, :, None] * b_im + ci[:, :, None] * b_re
    bd_in = lambda w: jnp.einsum("gph,ga->ghap", w, eye).reshape(g * hw, g * p)
    win = jnp.concatenate([bd_in(w_re), bd_in(w_im)], axis=1).astype(MXU_DTYPE)
    bd_out = lambda w: jnp.einsum("ghp,ga->gpah", w, eye).reshape(g * p, g * hw)
    cout = jnp.stack([bd_out(c_re.astype(F32)), -bd_out(c_im.astype(F32))]).astype(MXU_DTYPE)
    pr, pi = ar.reshape(1, g * p), ai.reshape(1, g * p)
    steps = S5_TB // SUBLANES
    while pr.shape[0] < steps:
        tr, ti = pr[-1:], pi[-1:]
        pr, pi = (jnp.concatenate([pr, pr * tr - pi * ti], axis=0), jnp.concatenate([pi, pr * ti + pi * tr], axis=0))
    powers = jnp.stack([pr[:steps], pi[:steps]])
    abar = jnp.concatenate([pr[0:1], pi[0:1], pr[steps - 1:steps], pi[steps - 1:steps]], axis=0)
    return win, abar, powers, cout, d.reshape(1, g * hw).astype(F32), w_glu.astype(MXU_DTYPE)


def _s5(u, weights, batch, seq):
    win, abar, powers, cout, d, wglu = weights
    tb = S5_TB
    assert seq % tb == 0
    ns = S5_GROUPS * S5_STATE
    width = S5_GROUPS * S5_GROUP_WIDTH
    steps = tb // SUBLANES
    src = (np.arange(tb) % SUBLANES) * steps + np.arange(tb) // SUBLANES
    fwd = (src[:, None] == np.arange(tb)[None, :]).astype(np.float32)
    perm = jnp.asarray(np.stack([fwd, fwd.T]), dtype=MXU_DTYPE)
    u3 = u.reshape(batch, seq, width)
    blk = pl.BlockSpec((1, tb, width), lambda b, i: (b, i, 0))
    return pl.pallas_call(
        _s5_kernel,
        grid=(batch, seq // tb),
        in_specs=[blk] + [_spec(w) for w in (perm, win, abar, powers, cout, d, wglu)],
        out_specs=blk,
        out_shape=jax.ShapeDtypeStruct((batch, seq, width), MXU_DTYPE),
        scratch_shapes=[pltpu.VMEM((tb, ns), F32), pltpu.VMEM((tb, ns), F32), pltpu.VMEM((SUBLANES, ns), F32)],
        compiler_params=_cparams("parallel", "arbitrary"),
    )(u3, *map(_arr, (perm, win, abar, powers, cout, d, wglu)))


def _merge_kernel(x_ref, yn_ref, yr_ref, ys_ref, mg_ref, wun_ref, wur_ref, wus_ref, wout_ref, g_ref, o_ref):
    d = x_ref.shape[1]
    merged = (mg_ref[:, 0:d] * _dot(yn_ref[...], wun_ref[...]) + mg_ref[:, d:2 * d] * _dot(yr_ref[...], wur_ref[...])
              + mg_ref[:, 2 * d:3 * d] * _dot(ys_ref[...], wus_ref[...]))
    o_ref[...] = x_ref[...] + _rms(_dot(merged, wout_ref[...]), g_ref[...])


def _merge(x2, y_nsa, y_rwkv, y_s5, mg, wun, wur, wus, wout, gain, tm):
    m, d = x2.shape
    row = lambda a: pl.BlockSpec((tm, a.shape[1]), lambda i: (i, 0))
    return pl.pallas_call(
        _merge_kernel,
        grid=(m // tm,),
        in_specs=[row(x2), row(y_nsa), row(y_rwkv), row(y_s5), row(mg)]
        + [_spec(w) for w in (wun, wur, wus, wout, gain)],
        out_specs=row(x2),
        out_shape=jax.ShapeDtypeStruct((m, d), F32),
        compiler_params=_cparams("parallel"),
    )(x2, y_nsa, y_rwkv, y_s5, mg, *map(_arr, (wun, wur, wus, wout, gain)))


def _memkv_kernel(m_ref, g_ref, w_ref, o_ref):
    o_ref[0] = _dot(_rms(m_ref[0], g_ref[...]), w_ref[...]).astype(o_ref.dtype)


def _memkv(mem, gain, wkv):
    b, n, d = mem.shape
    depth, _, w2 = wkv.shape
    return pl.pallas_call(
        _memkv_kernel,
        grid=(depth, b),
        in_specs=[pl.BlockSpec((1, n, d), lambda l, i: (i, 0, 0)), pl.BlockSpec((None, 1, d), lambda l, i: (l, 0, 0)),
                  pl.BlockSpec((None, d, w2), lambda l, i: (l, 0, 0))],
        out_specs=pl.BlockSpec((None, 1, n, w2), lambda l, i: (l, i, 0, 0)),
        out_shape=jax.ShapeDtypeStruct((depth, b, n, w2), MXU_DTYPE),
        compiler_params=_cparams("parallel", "parallel"),
    )(mem, gain, wkv)


def _xattn_kernel(x_ref, kv_ref, gin_ref, gout_ref, wq_ref, wo_ref, o_ref):
    x = x_ref[0]
    width = XA_HEADS * XA_HEAD_DIM
    q = (_dot(_rms(x, gin_ref[...]), wq_ref[...]) * (XA_HEAD_DIM ** -0.5)).astype(MXU_DTYPE)
    k = kv_ref[0, :, 0:width]
    v = kv_ref[0, :, width:2 * width]
    head = lax.broadcasted_iota(jnp.int32, q.shape, 1) // XA_HEAD_DIM
    att = jnp.zeros(q.shape, F32)
    for h in range(XA_HEADS):
        mine = head == h
        p = _softmax_last(_dot_nt(jnp.where(mine, q, jnp.zeros_like(q)), k))
        att = att + jnp.where(mine, _dot(p, v), 0.0)
    o_ref[0] = x + _rms(_dot(att, wo_ref[...]), gout_ref[...])


def _xattn(x3, kv, gin, gout, wq, wo, tm):
    b, s, d = x3.shape
    blk = pl.BlockSpec((1, tm, d), lambda i, j: (i, j, 0))
    return pl.pallas_call(
        _xattn_kernel,
        grid=(b, s // tm),
        in_specs=[blk, pl.BlockSpec((None, 1) + tuple(kv.shape[1:]), lambda i, j: tuple(kv.idx) + (i, 0, 0))]
        + [_spec(w) for w in (gin, gout, wq, wo)],
        out_specs=blk,
        out_shape=jax.ShapeDtypeStruct(x3.shape, F32),
        compiler_params=_cparams("parallel", "parallel"),
    )(x3, *map(_arr, (kv, gin, gout, wq, wo)))


FFN_CHUNK = 256


def _ffn_kernel(x_ref, gin_ref, gout_ref, wg_ref, wu_ref, wd_ref, o_ref):
    x = x_ref[...]
    h = _rms(x, gin_ref[...]).astype(MXU_DTYPE)
    acc = jnp.zeros(x.shape, F32)
    for c in range(wg_ref.shape[1] // FFN_CHUNK):
        sl = slice(c * FFN_CHUNK, (c + 1) * FFN_CHUNK)
        a = jnp.dot(h, wg_ref[:, sl], preferred_element_type=F32)
        b = jnp.dot(h, wu_ref[:, sl], preferred_element_type=F32)
        acc = acc + _dot(a * _sigmoid(a) * b, wd_ref[sl, :])
    o_ref[...] = x + _rms(acc, gout_ref[...])


def _ffn(x2, gin, gout, wg, wu, wd, tm):
    m, d = x2.shape
    assert wg.shape[1] % FFN_CHUNK == 0
    row = pl.BlockSpec((tm, d), lambda i: (i, 0))
    return pl.pallas_call(
        _ffn_kernel,
        grid=(m // tm,),
        in_specs=[row] + [_spec(w) for w in (gin, gout, wg, wu, wd)],
        out_specs=row,
        out_shape=jax.ShapeDtypeStruct((m, d), F32),
        compiler_params=_cparams("parallel"),
    )(x2, *map(_arr, (gin, gout, wg, wu, wd)))


ROW_TILE = 512
INPROJ_ROW_TILE = 256


def kernel(x, mem, norm_gains, mem_norm, w_in, nsa_cmp_pos_k, nsa_cmp_pos_v, nsa_ck_w1, nsa_ck_w2, nsa_cv_w1, nsa_cv_w2, rwkv_mu, rwkv_w0, rwkv_w2, rwkv_a0, rwkv_a2, rwkv_g2, rwkv_k_k, rwkv_k_a, rwkv_r_k, rwkv_ln_w, rwkv_ln_b, s5_lam_re, s5_lam_im, s5_log_dt, s5_b_re, s5_b_im, s5_c_re, s5_c_im, s5_d, s5_w_glu, w_up_nsa, w_up_rwkv, w_up_s5, w_out, xa_w_q, xa_w_k, xa_w_v, xa_w_o, ffn_w_gate, ffn_w_up, ffn_w_down):
    batch, seq, d = x.shape
    depth = w_in.shape[0]
    tm = min(ROW_TILE, seq)
    bf = lambda a: a.astype(MXU_DTYPE)

    half = NSA_HEAD_DIM // 2
    inv = 1.0 / (ROPE_THETA ** (jnp.arange(0, NSA_HEAD_DIM, 2, dtype=F32) / NSA_HEAD_DIM))
    ang = jnp.arange(seq, dtype=F32)[:, None] * inv[None, :]
    cos, sin = jnp.cos(ang), jnp.sin(ang)
    cos128 = jnp.tile(cos, (1, LANES // half))
    sin128 = jnp.tile(jnp.concatenate([-sin, sin], axis=1), (1, LANES // NSA_HEAD_DIM))

    gains = norm_gains.reshape(depth, norm_gains.shape[1], 1, d)
    w_all = jax.vmap(_regroup_w_in)(w_in)
    cmp_k = jax.vmap(_compress_weights)(nsa_cmp_pos_k, nsa_ck_w1, nsa_ck_w2)
    cmp_v = jax.vmap(_compress_weights)(nsa_cmp_pos_v, nsa_cv_w1, nsa_cv_w2)
    width = RWKV_HEADS * RWKV_HEAD_DIM
    zero = jnp.zeros((depth, rwkv_w2.shape[1], width), F32)
    wa2 = bf(jnp.stack([jnp.concatenate([rwkv_w2, zero], axis=1), jnp.concatenate([zero, rwkv_a2], axis=1)], axis=1))
    vecs = jnp.stack([rwkv_w0, rwkv_a0, rwkv_k_k, rwkv_k_a, rwkv_r_k.reshape(depth, width), rwkv_ln_w, rwkv_ln_b,
                      jnp.zeros((depth, width), F32)], axis=1)
    mu = rwkv_mu.reshape(depth, 1, -1)
    g2 = bf(rwkv_g2)
    s5w = jax.vmap(_s5_weights)(s5_lam_re, s5_lam_im, s5_log_dt, s5_b_re, s5_b_im, s5_c_re, s5_c_im, s5_d, s5_w_glu)
    wun, wur, wus, wout = bf(w_up_nsa), bf(w_up_rwkv), bf(w_up_s5), bf(w_out)
    wq, wo = bf(xa_w_q), bf(xa_w_o)
    wg, wu, wd = bf(ffn_w_gate), bf(ffn_w_up), bf(ffn_w_down)
    kv_all = _memkv(mem, mem_norm.reshape(depth, 1, d), bf(jnp.concatenate([xa_w_k, xa_w_v], axis=-1)))

    x2 = x.reshape(batch * seq, d)
    for l in range(depth):
        of = lambda a: _Of(a, l)
        gain = lambda i: _Of(gains, l, i)
        z = _inproj(x2, gain(0), cos128, sin128, of(w_all), seq, min(INPROJ_ROW_TILE, seq))

        chunks = lambda a: a.reshape(batch, seq // NSA_CMP_STRIDE, NSA_CMP_STRIDE * a.shape[-1])
        kc2, vc2 = _compress(chunks(z["kc"]), chunks(z["vc"]), *map(of, cmp_k), *map(of, cmp_v))
        y_nsa = _nsa(z, kc2, vc2, batch, seq)
        y_rwkv = _rwkv(z["rw"], of(mu), of(vecs), of(wa2), of(g2), batch, seq)
        y_s5 = _s5(z["s5"], tuple(map(of, s5w)), batch, seq)

        flat = lambda a: a.reshape(batch * seq, a.shape[-1])
        x2 = _merge(x2, flat(y_nsa), flat(y_rwkv), flat(y_s5), z["mg"], of(wun), of(wur), of(wus), of(wout), gain(1), tm)
        x2 = _xattn(x2.reshape(batch, seq, d), of(kv_all), gain(2), gain(3), of(wq), of(wo), tm).reshape(batch * seq, d)
        x2 = _ffn(x2, gain(4), gain(5), of(wg), of(wu), of(wd), tm)
    return x2.reshape(batch, seq, d)
```

```python
import functools
import math

import jax
import jax.numpy as jnp
import numpy as np
from jax import lax
from jax.experimental import pallas as pl
from jax.experimental.pallas import tpu as pltpu

NSA_HEADS = 8
NSA_KV_GROUPS = 2
NSA_HEAD_DIM = 64
NSA_CMP_LEN = 32
NSA_CMP_STRIDE = 16
NSA_SEL_LEN = 64
NSA_SEL_TOPK = 16
NSA_WINDOW = 512
RWKV_HEADS = 4
RWKV_HEAD_DIM = 64
RWKV_LN_EPS = 64e-5
S5_GROUPS = 16
S5_GROUP_WIDTH = 16
S5_STATE = 64
XA_HEADS = 4
XA_HEAD_DIM = 64
ROPE_THETA = 10000.0
NORM_EPS = 1e-6
NEG_INF = -1e30
SEL_FORCE_SCORE = 1e9

LANES = 128
SUBLANES = 8
MXU_DTYPE = jnp.bfloat16
VMEM_LIMIT = 56 << 20

F32 = jnp.float32


def _cparams(*sem):
    return pltpu.CompilerParams(dimension_semantics=sem, vmem_limit_bytes=VMEM_LIMIT)


class _Of:
    def __init__(self, arr, *idx):
        self.arr, self.idx = arr, idx

    @property
    def shape(self):
        return self.arr.shape[len(self.idx):]


def _arr(w):
    return w.arr if isinstance(w, _Of) else w


def _spec(w):
    nd = len(w.shape)
    if isinstance(w, _Of):
        lead = tuple(w.idx)
        return pl.BlockSpec((None,) * len(lead) + tuple(w.shape), lambda *_: lead + (0,) * nd,
                            pipeline_mode=pl.Buffered(1))
    return pl.BlockSpec(w.shape, lambda *_: (0,) * nd, pipeline_mode=pl.Buffered(1))


def _dot(a, b):
    return jnp.dot(a.astype(MXU_DTYPE), b.astype(MXU_DTYPE), preferred_element_type=F32)


def _dot_nt(a, b):
    return lax.dot_general(a.astype(MXU_DTYPE), b.astype(MXU_DTYPE), (((1,), (1,)), ((), ())),
                           preferred_element_type=F32)


def _split(a, terms):
    parts = []
    for _ in range(terms - 1):
        p = a.astype(MXU_DTYPE)
        parts.append(p)
        a = a - p.astype(F32)
    parts.append(a.astype(MXU_DTYPE))
    return parts


def _dot_split_lhs(a, b01, terms=3):
    b01 = b01.astype(MXU_DTYPE)
    return sum(jnp.dot(p, b01, preferred_element_type=F32) for p in _split(a, terms))


def _dot_split_rhs(a01, b, terms=3):
    a01 = a01.astype(MXU_DTYPE)
    return sum(jnp.dot(a01, p, preferred_element_type=F32) for p in _split(b, terms))


def _rms(x, g):
    return x * lax.rsqrt(jnp.mean(x * x, axis=-1, keepdims=True) + NORM_EPS) * g


def _sigmoid(x):
    return 1.0 / (1.0 + jnp.exp(-x))


def _gelu(x):
    return 0.5 * x * (1.0 + jnp.tanh(math.sqrt(2.0 / math.pi) * (x + 0.044715 * (x * x * x))))


def _softmax_last(s):
    m = jnp.max(s, axis=-1, keepdims=True)
    e = jnp.exp(s - m)
    return e / jnp.sum(e, axis=-1, keepdims=True)


_SEG = dict(q=(0, 512), kc=(512, 640), vc=(640, 768), ks=(768, 1024), vs=(1024, 1280), kw=(1280, 1536),
            vw=(1536, 1792), ng=(1792, 1920), rw=(1920, 2944), s5=(2944, 3200), mg=(3200, 6272))
_W_IN_COLS = 6272


def _rope128(x, cos, sin_signed):
    lane = lax.broadcasted_iota(jnp.int32, x.shape, 1)
    first = (lane % NSA_HEAD_DIM) < (NSA_HEAD_DIM // 2)
    rot = jnp.where(first, pltpu.roll(x, LANES - NSA_HEAD_DIM // 2, 1), pltpu.roll(x, NSA_HEAD_DIM // 2, 1))
    return x * cos + rot * sin_signed


def _inproj_kernel(x_ref, g_ref, cos_ref, sin_ref, w_ref,
                   qc_o, qr_o, kc_o, vc_o, ks_o, vs_o, kw_o, vw_o, ng_o, rw_o, s5_o, mg_o, *, n_tab):
    h = _rms(x_ref[...], g_ref[...]).astype(MXU_DTYPE)
    cos = cos_ref[...]
    sin = sin_ref[...]
    scale = NSA_HEAD_DIM ** -0.5 * math.log2(math.e)

    def seg(name):
        a, b = _SEG[name]
        return jnp.dot(h, w_ref[:, a:b], preferred_element_type=F32)

    q = seg("q")
    qc_o[...] = (q * scale).astype(qc_o.dtype)
    for c in range(q.shape[1] // LANES):
        sl = slice(c * LANES, (c + 1) * LANES)
        qr_o[:, sl] = (_rope128(q[:, sl], cos, sin) * scale).astype(qr_o.dtype)
    kc_o[...] = seg("kc")
    vc_o[...] = seg("vc")
    k = seg("kw")
    for c in range(NSA_KV_GROUPS):
        sl = slice(c * LANES, (c + 1) * LANES)
        kw_o[:, sl] = _rope128(k[:, sl], cos, sin).astype(kw_o.dtype)
    tm = x_ref.shape[0]
    tok = (pl.program_id(0) % n_tab) * tm + lax.broadcasted_iota(jnp.int32, (tm, LANES), 0)
    onehot = jnp.where(lax.broadcasted_iota(jnp.int32, (tm, LANES), 1) == tok // NSA_SEL_LEN, 1.0, 0.0)
    k = seg("ks")
    for c in range(NSA_KV_GROUPS):
        ks_o[:, 2 * c * LANES:(2 * c + 1) * LANES] = _rope128(k[:, c * LANES:(c + 1) * LANES], cos, sin).astype(ks_o.dtype)
        ks_o[:, (2 * c + 1) * LANES:(2 * c + 2) * LANES] = onehot.astype(ks_o.dtype)
    ones = jnp.ones((tm, LANES), vs_o.dtype)
    for name, out in (("vs", vs_o), ("vw", vw_o)):
        val = seg(name)
        for c in range(NSA_KV_GROUPS):
            out[:, 2 * c * LANES:(2 * c + 1) * LANES] = val[:, c * LANES:(c + 1) * LANES].astype(out.dtype)
            out[:, (2 * c + 1) * LANES:(2 * c + 2) * LANES] = ones
    ng_o[...] = _sigmoid(seg("ng"))
    rw_o[...] = seg("rw")
    s5_o[...] = seg("s5")
    mg_o[...] = _sigmoid(seg("mg")).astype(mg_o.dtype)


def _dup_groups(w):
    hd = NSA_HEAD_DIM
    return jnp.concatenate([w[:, :hd], w[:, :hd], w[:, hd:], w[:, hd:]], axis=1)


def _regroup_w_in(w):
    d = w.shape[0]
    q_w = NSA_HEADS * NSA_HEAD_DIM
    kv_w = NSA_KV_GROUPS * NSA_HEAD_DIM
    o = 0
    wq = w[:, o:o + q_w]; o += q_w
    kv = [w[:, o + i * kv_w:o + (i + 1) * kv_w] for i in range(6)]; o += 6 * kv_w
    wng = w[:, o:o + 3 * NSA_HEADS]; o += 3 * NSA_HEADS
    rw_w = 3 * RWKV_HEADS * RWKV_HEAD_DIM + 256
    wrw = w[:, o:o + rw_w]; o += rw_w
    s5_w = S5_GROUPS * S5_GROUP_WIDTH
    ws5 = w[:, o:o + s5_w]; o += s5_w
    wmg = w[:, o:]
    wng = jnp.pad(wng, ((0, 0), (0, LANES - wng.shape[1])))
    out = jnp.concatenate([wq, kv[0], kv[1], _dup_groups(kv[2]), _dup_groups(kv[3]), _dup_groups(kv[4]),
                           _dup_groups(kv[5]), wng, wrw, ws5, wmg], axis=1)
    assert out.shape == (d, _W_IN_COLS), out.shape
    return out.astype(MXU_DTYPE)


def _inproj(x2, gain, cos128, sin128, w_all, seq, tm):
    m, d = x2.shape
    n_tab = seq // tm
    row = lambda w: pl.BlockSpec((tm, w), lambda i: (i, 0))
    tab = pl.BlockSpec((tm, LANES), lambda i: (i % n_tab, 0))
    widths = dict(qc=512, qr=512, kc=128, vc=128, ks=512, vs=512, kw=256, vw=512, ng=128, rw=1024, s5=256, mg=3072)
    dtypes = dict(qc=MXU_DTYPE, qr=MXU_DTYPE, kc=F32, vc=F32, ks=MXU_DTYPE, vs=MXU_DTYPE, kw=MXU_DTYPE,
                  vw=MXU_DTYPE, ng=F32, rw=F32, s5=F32, mg=MXU_DTYPE)
    names = list(widths)
    outs = pl.pallas_call(
        functools.partial(_inproj_kernel, n_tab=n_tab),
        grid=(m // tm,),
        in_specs=[row(d), _spec(gain), tab, tab, _spec(w_all)],
        out_specs=[row(widths[n]) for n in names],
        out_shape=[jax.ShapeDtypeStruct((m, widths[n]), dtypes[n]) for n in names],
        compiler_params=_cparams("parallel"),
    )(x2, _arr(gain), cos128, sin128, _arr(w_all))
    return dict(zip(names, outs))


def _compress_kernel(k_ref, v_ref, pk_ref, pv_ref, kw1_ref, kw2_ref, vw1_ref, vw2_ref, kc_o, vc_o):
    def one(x_ref, p_ref, w1_ref, w2_ref):
        x = x_ref[0]
        n = x.shape[0]
        top = _dot(x + p_ref[0:1, :], w1_ref[0])
        bot = _dot(x + p_ref[1:2, :], w1_ref[1])
        hid = top + pltpu.roll(bot, n - 1, 0)
        return _dot(_gelu(hid), w2_ref[...])

    kc_o[0] = one(k_ref, pk_ref, kw1_ref, kw2_ref).astype(kc_o.dtype)
    vc = one(v_ref, pv_ref, vw1_ref, vw2_ref)
    for c in range(NSA_KV_GROUPS):
        vc_o[0, :, 2 * c * LANES:(2 * c + 1) * LANES] = vc[:, c * LANES:(c + 1) * LANES].astype(vc_o.dtype)
        vc_o[0, :, (2 * c + 1) * LANES:(2 * c + 2) * LANES] = jnp.ones((vc.shape[0], LANES), vc_o.dtype)


def _compress_weights(pos, w1, w2):
    g, hd, half = NSA_KV_GROUPS, NSA_HEAD_DIM, NSA_CMP_STRIDE
    hidden = w1.shape[1]
    eye = jnp.eye(g, dtype=w1.dtype)
    w1r = w1.reshape(2, half, hd, hidden)
    w1e = jnp.einsum("pldj,ab->pladbj", w1r, eye).reshape(2, half * g * hd, g * hidden)
    w2d = jnp.concatenate([w2, w2], axis=1)
    w2e = jnp.einsum("jd,ab->ajbd", w2d, eye).reshape(g * hidden, g * 2 * hd)
    pos_e = jnp.broadcast_to(pos.reshape(2, half, 1, hd), (2, half, g, hd)).reshape(2, half * g * hd)
    return pos_e.astype(F32), w1e.astype(MXU_DTYPE), w2e.astype(MXU_DTYPE)


def _compress(kc, vc, pk, kw1, kw2, pv, vw1, vw2):
    b, n, width = kc.shape
    blk = pl.BlockSpec((1, n, width), lambda i: (i, 0, 0))
    out_w = kw2.shape[1]
    out_blk = lambda w: pl.BlockSpec((1, n, w), lambda i: (i, 0, 0))
    return pl.pallas_call(
        _compress_kernel,
        grid=(b,),
        in_specs=[blk, blk] + [_spec(w) for w in (pk, pv, kw1, kw2, vw1, vw2)],
        out_specs=[out_blk(out_w), out_blk(2 * out_w)],
        out_shape=[jax.ShapeDtypeStruct((b, n, out_w), MXU_DTYPE), jax.ShapeDtypeStruct((b, n, 2 * out_w), MXU_DTYPE)],
        compiler_params=_cparams("parallel"),
    )(kc, vc, *map(_arr, (pk, pv, kw1, kw2, vw1, vw2)))


NSA_TQ = 128
NSA_TK = 1024
NSA_TK_STEP = 256
MASKED = -1e30


def _nsa_kernel(qc_ref, qr_ref, kc_ref, vc_ref, ks_ref, vs_ref, kw_ref, vw_ref, ng_ref, ovl_ref, gexp_ref, y_ref, *, top_k):
    tq = qc_ref.shape[1]
    seq = ks_ref.shape[1]
    n_cmp_pad = kc_ref.shape[1]
    hpg = NSA_HEADS // NSA_KV_GROUPS
    rows = hpg * tq
    n_sel = seq // NSA_SEL_LEN
    n_sel_pad = -(-n_sel // SUBLANES) * SUBLANES
    t0 = pl.program_id(1) * tq
    tcol = t0 + lax.broadcasted_iota(jnp.int32, (tq, 1), 0)
    low_half = lax.broadcasted_iota(jnp.int32, (tq, LANES), 1) < NSA_HEAD_DIM

    def stack_heads(q_ref, g):
        parts = []
        for r in range(hpg):
            h = g * hpg + r
            slab = q_ref[0, :, (h // 2) * LANES:(h // 2 + 1) * LANES]
            keep = low_half if h % 2 == 0 else jnp.logical_not(low_half)
            parts.append(jnp.where(keep, slab, jnp.zeros_like(slab)))
        return jnp.concatenate(parts, axis=0)

    def unstack(o, g, y_parts):
        for pair in range(hpg // 2):
            even = o[(2 * pair) * tq:(2 * pair + 1) * tq]
            odd = o[(2 * pair + 1) * tq:(2 * pair + 2) * tq]
            y_parts[g * (hpg // 2) + pair] = jnp.where(low_half, even, odd)

    def masked_exp_pv(q, k, bias, v_ones):
        s = [_dot_nt(q[g], k[g]) for g in groups]
        n = s[0].shape[-1]
        s = [sg.reshape(hpg, tq, n) + bias[None] for sg in s]
        e = [jnp.exp2(sg - jnp.max(sg, axis=-1, keepdims=True)) for sg in s]
        return e, [_dot(e[g].reshape(rows, n), v_ones[g]) for g in groups]

    cmp_end = lax.broadcasted_iota(jnp.int32, (1, n_cmp_pad), 1) * NSA_CMP_STRIDE + (NSA_CMP_LEN - 1)
    cmp_bias = jnp.where(cmp_end <= tcol, 0.0, NEG_INF)
    cmp_live = jnp.where(tcol >= NSA_CMP_LEN - 1, 1.0, 0.0)
    slab = NSA_WINDOW + tq
    w0 = pl.multiple_of(jnp.maximum(t0 - NSA_WINDOW, 0), tq)
    wpos = w0 + lax.broadcasted_iota(jnp.int32, (1, slab), 1)
    win_bias = jnp.where(wpos <= tcol, jnp.where(wpos > tcol - NSA_WINDOW, 0.0, NEG_INF), NEG_INF)
    n_kt = (t0 + tq - 1) // NSA_TK + 1
    k_last = pl.multiple_of((n_kt - 1) * NSA_TK, NSA_TK)

    blk = lax.broadcasted_iota(jnp.int32, (n_sel_pad, tq), 0)
    cur = (t0 + lax.broadcasted_iota(jnp.int32, (n_sel_pad, tq), 1)) // NSA_SEL_LEN
    forced = (blk == 0) | (blk == cur) | (blk == cur - 1)

    groups = range(NSA_KV_GROUPS)
    gl = [slice(g * LANES, (g + 1) * LANES) for g in groups]
    gx = [slice(2 * g * LANES, (2 * g + 2) * LANES) for g in groups]
    y_cmp, y_sel, y_win = [None] * 4, [None] * 4, [None] * 4
    qc = [stack_heads(qc_ref, g) for g in groups]
    qr = [stack_heads(qr_ref, g) for g in groups]

    cmp_e, cmp_pv = masked_exp_pv(qc, [kc_ref[0, :, gl[g]] for g in groups], cmp_bias, [vc_ref[0, :, gx[g]] for g in groups])
    cmp_inv = [(cmp_live[None] / cmp_pv[g][:, LANES:].reshape(hpg, tq, LANES)) for g in groups]
    for g in groups:
        unstack(cmp_pv[g][:, :LANES] * cmp_inv[g].reshape(rows, LANES), g, y_cmp)

    _, win = masked_exp_pv(qr, [kw_ref[0, pl.ds(w0, slab), gl[g]] for g in groups], win_bias,
                           [vw_ref[0, pl.ds(w0, slab), gx[g]] for g in groups])
    for g in groups:
        unstack(win[g][:, :LANES] / win[g][:, LANES:], g, y_win)

    qx = []
    for g in groups:
        p = cmp_e[g] * jnp.concatenate([cmp_inv[g]] * -(-n_cmp_pad // LANES), axis=-1)[..., :n_cmp_pad]
        imp = _dot_split_lhs(jnp.sum(p, axis=0), ovl_ref[...])
        vals = jnp.where(blk <= cur, jnp.where(forced, SEL_FORCE_SCORE, imp.T[:n_sel_pad]), NEG_INF)
        rank = [jnp.zeros((SUBLANES, tq), F32) for _ in range(n_sel_pad // SUBLANES)]
        for i in range(n_sel):
            other = vals[i:i + 1, :]
            for u in range(n_sel_pad // SUBLANES):
                mine = vals[u * SUBLANES:(u + 1) * SUBLANES]
                if i < u * SUBLANES:
                    ahead = other >= mine
                elif i >= (u + 1) * SUBLANES:
                    ahead = other > mine
                else:
                    ahead = (other > mine) | ((other == mine) & (blk[u * SUBLANES:(u + 1) * SUBLANES] > i))
                rank[u] = rank[u] + jnp.where(ahead, 1.0, 0.0)
        drop = jnp.where(jnp.concatenate(rank, axis=0) < top_k, 0.0, MASKED)
        drop = jnp.concatenate([drop, jnp.zeros((LANES - n_sel_pad, tq), F32)], axis=0).T.astype(MXU_DTYPE)
        qx.append(jnp.concatenate([qr[g], jnp.concatenate([drop] * hpg, axis=0)], axis=1))

    def sel_tiles(k0, tk, causal, carries):
        s = [_dot_nt(qx[g], ks_ref[0, pl.ds(k0, tk), gx[g]]).reshape(hpg, tq, tk) for g in groups]
        if causal:
            bias = jnp.where(k0 + lax.broadcasted_iota(jnp.int32, (1, tk), 1) <= tcol, 0.0, NEG_INF)
            s = [sg + bias[None] for sg in s]
        m_new = [jnp.maximum(carries[g][0], jnp.max(s[g], axis=-1, keepdims=True)) for g in groups]
        e = [jnp.exp2(s[g] - m_new[g]) for g in groups]
        pv = [_dot(e[g].reshape(rows, tk), vs_ref[0, pl.ds(k0, tk), gx[g]]) for g in groups]
        return tuple((m_new[g], jnp.exp2(carries[g][0] - m_new[g]) * carries[g][1] + pv[g].reshape(hpg, tq, 2 * LANES))
                     for g in groups)

    init = (jnp.full((hpg, tq, 1), NEG_INF, F32), jnp.zeros((hpg, tq, 2 * LANES), F32))
    carries = lax.fori_loop(0, n_kt - 1, lambda j, c: sel_tiles(pl.multiple_of(j * NSA_TK, NSA_TK), NSA_TK, False, c),
                            tuple(init for _ in groups))
    widths = tuple(range(NSA_TK_STEP, NSA_TK + 1, NSA_TK_STEP))
    need = (t0 + tq - k_last + NSA_TK_STEP - 1) // NSA_TK_STEP - 1
    carries = lax.switch(need, [functools.partial(sel_tiles, k_last, w, True) for w in widths], carries)
    for g in groups:
        acc = carries[g][1].reshape(rows, 2 * LANES)
        unstack(acc[:, :LANES] / acc[:, LANES:], g, y_sel)

    gates = ng_ref[0]
    q_w = NSA_HEADS * NSA_HEAD_DIM
    spread = _dot_split_lhs(gates, gexp_ref[...], terms=2)
    for c in range(4):
        sl = slice(c * LANES, (c + 1) * LANES)
        y_ref[0, :, sl] = (spread[:, sl] * y_cmp[c] + spread[:, q_w + c * LANES:q_w + (c + 1) * LANES] * y_sel[c]
                           + spread[:, 2 * q_w + c * LANES:2 * q_w + (c + 1) * LANES] * y_win[c]).astype(y_ref.dtype)


def _nsa_constants(seq, n_cmp_pad):
    n_sel = seq // NSA_SEL_LEN
    cmp_start = np.arange(n_cmp_pad) * NSA_CMP_STRIDE
    sel_start = np.arange(LANES) * NSA_SEL_LEN
    ovl = ((cmp_start[:, None] < sel_start[None, :] + NSA_SEL_LEN) & (cmp_start[:, None] + NSA_CMP_LEN > sel_start[None, :])
           & (np.arange(LANES)[None, :] < n_sel) & (np.arange(n_cmp_pad)[:, None] < n_cmp_pad - 1))
    q_w = NSA_HEADS * NSA_HEAD_DIM
    gexp = np.zeros((LANES, 3 * q_w), np.float32)
    for i in range(3):
        for h in range(NSA_HEADS):
            gexp[3 * h + i, i * q_w + h * NSA_HEAD_DIM:i * q_w + (h + 1) * NSA_HEAD_DIM] = 1.0
    return jnp.asarray(ovl.astype(np.float32), dtype=MXU_DTYPE), jnp.asarray(gexp, dtype=MXU_DTYPE)


def _nsa(z, kc2, vc2, batch, seq):
    tq = NSA_TQ
    q_w = NSA_HEADS * NSA_HEAD_DIM
    n_cmp_pad = kc2.shape[1]
    ovl, gexp = _nsa_constants(seq, n_cmp_pad)
    r3 = lambda a: a.reshape(batch, seq, a.shape[-1])
    qblk = pl.BlockSpec((1, tq, q_w), lambda b, i: (b, i, 0))
    full = lambda n, w: pl.BlockSpec((1, n, w), lambda b, i: (b, 0, 0))
    top_k = min(NSA_SEL_TOPK, seq // NSA_SEL_LEN)
    return pl.pallas_call(
        functools.partial(_nsa_kernel, top_k=top_k),
        grid=(batch, seq // tq),
        in_specs=[qblk, qblk, full(n_cmp_pad, 256), full(n_cmp_pad, 512), full(seq, 512), full(seq, 512),
                  full(seq, 256), full(seq, 512), pl.BlockSpec((1, tq, LANES), lambda b, i: (b, i, 0)),
                  _spec(ovl), _spec(gexp)],
        out_specs=qblk,
        out_shape=jax.ShapeDtypeStruct((batch, seq, q_w), MXU_DTYPE),
        compiler_params=_cparams("parallel", "arbitrary"),
    )(r3(z["qc"]), r3(z["qr"]), kc2, vc2, r3(z["ks"]), r3(z["vs"]), r3(z["kw"]), r3(z["vw"]), r3(z["ng"]),
      ovl, gexp)


RWKV_CHUNK = 64
RWKV_TB = 512


def _rwkv_kernel(z_ref, mu_ref, vecs_ref, wa2_ref, g2_ref, ones_ref, o_ref, prev_ref, st_ref):
    tb = z_ref.shape[1]
    width = RWKV_HEADS * RWKV_HEAD_DIM
    n = RWKV_HEAD_DIM
    c = RWKV_CHUNK

    @pl.when(pl.program_id(1) == 0)
    def _():
        prev_ref[...] = jnp.zeros_like(prev_ref)
        st_ref[...] = jnp.zeros_like(st_ref)

    z = z_ref[0]
    row = lax.broadcasted_iota(jnp.int32, (tb, 1), 0)
    z_prev = jnp.where(row == 0, prev_ref[...], pltpu.roll(z, 1, 0))
    prev_ref[...] = z[tb - 1:tb, :]
    zs = z + (z_prev - z) * mu_ref[...]

    w0, a0, k_k, k_a, r_k, ln_w, ln_b = (vecs_ref[i:i + 1, :] for i in range(7))
    r = zs[:, 0:width]
    k = zs[:, width:2 * width]
    v = zs[:, 2 * width:3 * width]
    lr = zs[:, 3 * width:3 * width + LANES]
    w = w0 + _dot(jnp.tanh(lr), wa2_ref[0])
    w = -(jnp.maximum(-w, 0.0) + jnp.log(1.0 + jnp.exp(-jnp.abs(w)))) - 0.5
    logd = -jnp.exp(w)
    a = _sigmoid(a0 + _dot(lr, wa2_ref[1]))
    gate = _dot(_sigmoid(zs[:, 3 * width + LANES:]), g2_ref[...])
    ones_h = ones_ref[...]
    kk = k * k_k
    kk = kk / jnp.maximum(jnp.sqrt(_dot_split_lhs(kk * kk, ones_h, terms=2)), 1e-12)
    k = k * (1.0 + (a - 1.0) * k_a)
    bonus = _dot_split_lhs(r * k * r_k, ones_h, terms=2) * v

    ri = lax.broadcasted_iota(jnp.int32, (tb, tb), 0)
    ci = lax.broadcasted_iota(jnp.int32, (tb, tb), 1)
    same = (ri // c) == (ci // c)
    cl = _dot_split_rhs(jnp.where(same & (ci <= ri), 1.0, 0.0), logd)
    cend = jnp.concatenate([jnp.broadcast_to(cl[q * c + c - 1:q * c + c, :], (c, width)) for q in range(tb // c)],
                           axis=0)
    e_inv = jnp.exp(-cl)
    e_end = jnp.exp(cend - cl)
    a_t = -kk * jnp.exp(cl - logd)
    r_t = r * jnp.exp(cl)
    b = kk * a
    b_t = b * e_inv
    k_t = k * e_inv
    b_end_T = (b * e_end).T
    k_end_T = (k * e_end).T
    w_end = jnp.exp(cend)

    head_of_row = lax.broadcasted_iota(jnp.int32, (width, width), 0) // n
    head_of_lane = lax.broadcasted_iota(jnp.int32, (width, width), 1) // n
    on_diag_block = head_of_row == head_of_lane
    eye_w = lax.broadcasted_iota(jnp.int32, (width, width), 0) == lax.broadcasted_iota(jnp.int32, (width, width), 1)
    step = lax.broadcasted_iota(jnp.int32, (c, width), 0)
    within = lax.broadcasted_iota(jnp.int32, (c, width), 1) % n
    strict = within < step
    incl = within <= step
    eye_c = jnp.where(within == step, 1.0, 0.0)

    def bd(m):
        m = m.astype(MXU_DTYPE)
        return jnp.where(on_diag_block, jnp.concatenate([m] * RWKV_HEADS, axis=0), jnp.zeros((), MXU_DTYPE))

    def chunk_maps(chunk_ids):
        qs = [slice(q * c, (q + 1) * c) for q in chunk_ids]
        ids = range(len(qs))
        xs = [jnp.concatenate([a_t[s], r_t[s]], axis=0) for s in qs]
        pb = [_dot_nt(xs[i], bd(b_t[qs[i]])) for i in ids]
        pk = [_dot_nt(xs[i], bd(k_t[qs[i]])) for i in ids]
        n_low = [jnp.where(strict, p[:c], 0.0) for p in pb]
        t_inv = [eye_c + jnp.where((step // 2) == (within // 2), m, 0.0) for m in n_low]
        size = 2
        while size < c:
            off = ((step // (2 * size)) == (within // (2 * size))) & ((step // size) != (within // size))
            y = [_dot(jnp.where(off, m, 0.0), bd(t)) for m, t in zip(n_low, t_inv)]
            t_inv = [t + _dot(t, bd(yy)) for t, yy in zip(t_inv, y)]
            size *= 2
        v_bd = [bd(v[s]) for s in qs]
        nv = [_dot(jnp.where(strict, pk[i][:c], 0.0), v_bd[i]) for i in ids]
        mg1 = [_dot(t_inv[i], jnp.concatenate([bd(a_t[qs[i]]), bd(nv[i])], axis=1)) for i in ids]
        dm = [_dot(jnp.where(incl, pb[i][c:], 0.0), jnp.concatenate([bd(mg1[i][:, :width]), bd(mg1[i][:, width:])], axis=1))
              for i in ids]
        drk_v = [_dot(jnp.where(incl, pk[i][c:], 0.0), v_bd[i]) for i in ids]
        bm = [_dot(b_end_T[:, qs[i]], mg1[i]) for i in ids]
        kv_end = [_dot(k_end_T[:, qs[i]], v[qs[i]]) for i in ids]
        maps = []
        for i, q in enumerate(chunk_ids):
            m2 = r_t[qs[i]] + dm[i][:, :width]
            g2 = dm[i][:, width:] + drk_v[i]
            m3 = jnp.where(on_diag_block, bm[i][:, :width], 0.0) + jnp.where(eye_w, w_end[q * c:q * c + 1, :], 0.0)
            g3 = jnp.where(on_diag_block, bm[i][:, width:] + kv_end[i], 0.0)
            maps.append((m2, g2, m3, g3))
        return maps

    st = st_ref[...]
    ys = []
    for m2, g2, m3, g3 in chunk_maps(range(tb // c)):
        ys.append(_dot(m2, st) + g2)
        st = _dot(m3, st) + g3
    st_ref[...] = st
    y = jnp.concatenate(ys, axis=0)
    inv_n = 1.0 / n
    mean = _dot_split_lhs(y, ones_h, terms=2) * inv_n
    var = _dot_split_lhs(jnp.square(y - mean), ones_h, terms=2) * inv_n
    yn = (y - mean) * lax.rsqrt(var + RWKV_LN_EPS)
    o_ref[0] = (((yn * ln_w + ln_b) + bonus) * gate).astype(o_ref.dtype)


def _rwkv(z_rw, mu, vecs, wa2, g2, batch, seq):
    tb = min(RWKV_TB, seq)
    width = RWKV_HEADS * RWKV_HEAD_DIM
    head = np.arange(width) // RWKV_HEAD_DIM
    ones_h = jnp.asarray((head[:, None] == head[None, :]).astype(np.float32))
    z3 = z_rw.reshape(batch, seq, z_rw.shape[-1])
    return pl.pallas_call(
        _rwkv_kernel,
        grid=(batch, seq // tb),
        in_specs=[pl.BlockSpec((1, tb, z3.shape[-1]), lambda b, i: (b, i, 0))]
        + [_spec(w) for w in (mu, vecs, wa2, g2, ones_h)],
        out_specs=pl.BlockSpec((1, tb, width), lambda b, i: (b, i, 0)),
        out_shape=jax.ShapeDtypeStruct((batch, seq, width), MXU_DTYPE),
        scratch_shapes=[pltpu.VMEM((1, z3.shape[-1]), F32), pltpu.VMEM((width, width), F32)],
        compiler_params=_cparams("parallel", "arbitrary"),
    )(z3, *map(_arr, (mu, vecs, wa2, g2, ones_h)))


S5_TB = 512


def _s5_kernel(u_ref, perm_ref, win_ref, abar_ref, pow_ref, cout_ref, d_ref, wglu_ref, o_ref, xr_ref, xi_ref, carry_ref):
    tb = u_ref.shape[1]
    ns = S5_GROUPS * S5_STATE
    steps = tb // SUBLANES

    @pl.when(pl.program_id(1) == 0)
    def _():
        carry_ref[...] = jnp.zeros_like(carry_ref)

    u = u_ref[0]
    u_perm = jnp.dot(perm_ref[0], u.astype(MXU_DTYPE), preferred_element_type=F32)
    xr_ref[...] = _dot(u_perm, win_ref[:, :ns])
    xi_ref[...] = _dot(u_perm, win_ref[:, ns:])
    ar, ai = abar_ref[0:1, :], abar_ref[1:2, :]

    def scan_step(s, state):
        sr, si = state
        rows = pl.ds(pl.multiple_of(s * SUBLANES, SUBLANES), SUBLANES)
        nr = ar * sr - ai * si + xr_ref[rows, :]
        ni = ar * si + ai * sr + xi_ref[rows, :]
        xr_ref[rows, :] = nr
        xi_ref[rows, :] = ni
        return nr, ni

    zero = jnp.zeros((SUBLANES, ns), F32)
    fr, fi = lax.fori_loop(0, steps, scan_step, (zero, zero), unroll=4)

    cr, ci = abar_ref[2:3, :], abar_ref[3:4, :]
    er, ei = [carry_ref[0:1, :]], [carry_ref[1:2, :]]
    for c in range(SUBLANES):
        pr, pi = er[-1], ei[-1]
        er.append(fr[c:c + 1, :] + (cr * pr - ci * pi))
        ei.append(fi[c:c + 1, :] + (cr * pi + ci * pr))
    carry_ref[0:1, :] = er[SUBLANES]
    carry_ref[1:2, :] = ei[SUBLANES]
    enter_r = jnp.concatenate(er[:SUBLANES], axis=0)
    enter_i = jnp.concatenate(ei[:SUBLANES], axis=0)

    def add_entering(s, _):
        rows = pl.ds(pl.multiple_of(s * SUBLANES, SUBLANES), SUBLANES)
        pr, pi = pow_ref[0, pl.ds(s, 1), :], pow_ref[1, pl.ds(s, 1), :]
        xr_ref[rows, :] = xr_ref[rows, :] + (pr * enter_r - pi * enter_i)
        xi_ref[rows, :] = xi_ref[rows, :] + (pr * enter_i + pi * enter_r)
        return 0

    lax.fori_loop(0, steps, add_entering, 0, unroll=4)
    y_perm = _dot(xr_ref[...], cout_ref[0]) + _dot(xi_ref[...], cout_ref[1])
    y = _gelu(_dot_split_rhs(perm_ref[1], y_perm, terms=2) + d_ref[...] * u)
    o_ref[0] = (y * _sigmoid(_dot(y, wglu_ref[...]))).astype(o_ref.dtype)


def _s5_weights(lam_re, lam_im, log_dt, b_re, b_im, c_re, c_im, d, w_glu):
    g, p, hw = S5_GROUPS, S5_STATE, S5_GROUP_WIDTH
    dt = jnp.exp(log_dt.astype(F32))[:, None]
    lr, li = lam_re.astype(F32), lam_im.astype(F32)
    mag = jnp.exp(lr * dt)
    ar, ai = mag * jnp.cos(li * dt), mag * jnp.sin(li * dt)
    den = lr * lr + li * li
    cr = ((ar - 1.0) * lr + ai * li) / den
    ci = (ai * lr - (ar - 1.0) * li) / den
    eye = jnp.eye(g, dtype=F32)
    w_re = cr[:, :, None] * b_re - ci[:, :, None] * b_im
    w_im = cr[:, :, None] * b_im + ci[:, :, None] * b_re
    bd_in = lambda w: jnp.einsum("gph,ga->ghap", w, eye).reshape(g * hw, g * p)
    win = jnp.concatenate([bd_in(w_re), bd_in(w_im)], axis=1).astype(MXU_DTYPE)
    bd_out = lambda w: jnp.einsum("ghp,ga->gpah", w, eye).reshape(g * p, g * hw)
    cout = jnp.stack([bd_out(c_re.astype(F32)), -bd_out(c_im.astype(F32))]).astype(MXU_DTYPE)
    pr, pi = ar.reshape(1, g * p), ai.reshape(1, g * p)
    steps = S5_TB // SUBLANES
    while pr.shape[0] < steps:
        tr, ti = pr[-1:], pi[-1:]
        pr, pi = (jnp.concatenate([pr, pr * tr - pi * ti], axis=0), jnp.concatenate([pi, pr * ti + pi * tr], axis=0))
    powers = jnp.stack([pr[:steps], pi[:steps]])
    abar = jnp.concatenate([pr[0:1], pi[0:1], pr[steps - 1:steps], pi[steps - 1:steps]], axis=0)
    return win, abar, powers, cout, d.reshape(1, g * hw).astype(F32), w_glu.astype(MXU_DTYPE)


def _s5(u, weights, batch, seq):
    win, abar, powers, cout, d, wglu = weights
    tb = S5_TB
    assert seq % tb == 0
    ns = S5_GROUPS * S5_STATE
    width = S5_GROUPS * S5_GROUP_WIDTH
    steps = tb // SUBLANES
    src = (np.arange(tb) % SUBLANES) * steps + np.arange(tb) // SUBLANES
    fwd = (src[:, None] == np.arange(tb)[None, :]).astype(np.float32)
    perm = jnp.asarray(np.stack([fwd, fwd.T]), dtype=MXU_DTYPE)
    u3 = u.reshape(batch, seq, width)
    blk = pl.BlockSpec((1, tb, width), lambda b, i: (b, i, 0))
    return pl.pallas_call(
        _s5_kernel,
        grid=(batch, seq // tb),
        in_specs=[blk] + [_spec(w) for w in (perm, win, abar, powers, cout, d, wglu)],
        out_specs=blk,
        out_shape=jax.ShapeDtypeStruct((batch, seq, width), MXU_DTYPE),
        scratch_shapes=[pltpu.VMEM((tb, ns), F32), pltpu.VMEM((tb, ns), F32), pltpu.VMEM((SUBLANES, ns), F32)],
        compiler_params=_cparams("parallel", "arbitrary"),
    )(u3, *map(_arr, (perm, win, abar, powers, cout, d, wglu)))


def _memkv_kernel(m_ref, g_ref, w_ref, o_ref):
    o_ref[0] = _dot(_rms(m_ref[0], g_ref[...]), w_ref[...]).astype(o_ref.dtype)


def _memkv(mem, gain, wkv):
    b, n, d = mem.shape
    depth, _, w2 = wkv.shape
    return pl.pallas_call(
        _memkv_kernel,
        grid=(depth, b),
        in_specs=[pl.BlockSpec((1, n, d), lambda l, i: (i, 0, 0)), pl.BlockSpec((None, 1, d), lambda l, i: (l, 0, 0)),
                  pl.BlockSpec((None, d, w2), lambda l, i: (l, 0, 0))],
        out_specs=pl.BlockSpec((None, 1, n, w2), lambda l, i: (l, i, 0, 0)),
        out_shape=jax.ShapeDtypeStruct((depth, b, n, w2), MXU_DTYPE),
        compiler_params=_cparams("parallel", "parallel"),
    )(mem, gain, wkv)


FFN_CHUNK = 256


def _layer_tail_kernel(x_ref, yn_ref, yr_ref, ys_ref, mg_ref, kv_ref, gains_ref, wun_ref, wur_ref, wus_ref, wout_ref,
                       wq_ref, wo_ref, wg_ref, wu_ref, wd_ref, o_ref):
    x = x_ref[0]
    d = x.shape[1]
    gain = lambda i: gains_ref[i]

    merged = (mg_ref[0, :, 0:d] * _dot(yn_ref[0], wun_ref[...]) + mg_ref[0, :, d:2 * d] * _dot(yr_ref[0], wur_ref[...])
              + mg_ref[0, :, 2 * d:3 * d] * _dot(ys_ref[0], wus_ref[...]))
    x = x + _rms(_dot(merged, wout_ref[...]), gain(1))

    width = XA_HEADS * XA_HEAD_DIM
    q = (_dot(_rms(x, gain(2)), wq_ref[...]) * (XA_HEAD_DIM ** -0.5)).astype(MXU_DTYPE)
    k = kv_ref[0, :, 0:width]
    v = kv_ref[0, :, width:2 * width]
    head = lax.broadcasted_iota(jnp.int32, q.shape, 1) // XA_HEAD_DIM
    att = jnp.zeros(q.shape, F32)
    for h in range(XA_HEADS):
        mine = head == h
        p = _softmax_last(_dot_nt(jnp.where(mine, q, jnp.zeros_like(q)), k))
        att = att + jnp.where(mine, _dot(p, v), 0.0)
    x = x + _rms(_dot(att, wo_ref[...]), gain(3))

    h = _rms(x, gain(4)).astype(MXU_DTYPE)
    acc = jnp.zeros(x.shape, F32)
    for c in range(wg_ref.shape[1] // FFN_CHUNK):
        sl = slice(c * FFN_CHUNK, (c + 1) * FFN_CHUNK)
        a = jnp.dot(h, wg_ref[:, sl], preferred_element_type=F32)
        b = jnp.dot(h, wu_ref[:, sl], preferred_element_type=F32)
        acc = acc + _dot(a * _sigmoid(a) * b, wd_ref[sl, :])
    o_ref[0] = x + _rms(acc, gain(5))


def _layer_tail(x3, y_nsa, y_rwkv, y_s5, mg, kv, gains, weights, tm):
    b, s, d = x3.shape
    assert weights[-1].shape[0] % FFN_CHUNK == 0
    blk = lambda a: pl.BlockSpec((1, tm, a.shape[-1]), lambda i, j: (i, j, 0))
    kv_spec = pl.BlockSpec((None, 1) + tuple(kv.shape[1:]), lambda i, j: tuple(kv.idx) + (i, 0, 0))
    return pl.pallas_call(
        _layer_tail_kernel,
        grid=(b, s // tm),
        in_specs=[blk(x3), blk(y_nsa), blk(y_rwkv), blk(y_s5), blk(mg), kv_spec, _spec(gains)] + [_spec(w) for w in weights],
        out_specs=blk(x3),
        out_shape=jax.ShapeDtypeStruct(x3.shape, F32),
        compiler_params=_cparams("parallel", "parallel"),
    )(x3, y_nsa, y_rwkv, y_s5, mg, *map(_arr, (kv, gains) + tuple(weights)))


ROW_TILE = 512
INPROJ_ROW_TILE = 256


def kernel(x, mem, norm_gains, mem_norm, w_in, nsa_cmp_pos_k, nsa_cmp_pos_v, nsa_ck_w1, nsa_ck_w2, nsa_cv_w1, nsa_cv_w2, rwkv_mu, rwkv_w0, rwkv_w2, rwkv_a0, rwkv_a2, rwkv_g2, rwkv_k_k, rwkv_k_a, rwkv_r_k, rwkv_ln_w, rwkv_ln_b, s5_lam_re, s5_lam_im, s5_log_dt, s5_b_re, s5_b_im, s5_c_re, s5_c_im, s5_d, s5_w_glu, w_up_nsa, w_up_rwkv, w_up_s5, w_out, xa_w_q, xa_w_k, xa_w_v, xa_w_o, ffn_w_gate, ffn_w_up, ffn_w_down):
    batch, seq, d = x.shape
    depth = w_in.shape[0]
    tm = min(ROW_TILE, seq)
    bf = lambda a: a.astype(MXU_DTYPE)

    half = NSA_HEAD_DIM // 2
    inv = 1.0 / (ROPE_THETA ** (jnp.arange(0, NSA_HEAD_DIM, 2, dtype=F32) / NSA_HEAD_DIM))
    ang = jnp.arange(seq, dtype=F32)[:, None] * inv[None, :]
    cos, sin = jnp.cos(ang), jnp.sin(ang)
    cos128 = jnp.tile(cos, (1, LANES // half))
    sin128 = jnp.tile(jnp.concatenate([-sin, sin], axis=1), (1, LANES // NSA_HEAD_DIM))

    gains = norm_gains.reshape(depth, norm_gains.shape[1], 1, d)
    w_all = jax.vmap(_regroup_w_in)(w_in)
    cmp_k = jax.vmap(_compress_weights)(nsa_cmp_pos_k, nsa_ck_w1, nsa_ck_w2)
    cmp_v = jax.vmap(_compress_weights)(nsa_cmp_pos_v, nsa_cv_w1, nsa_cv_w2)
    width = RWKV_HEADS * RWKV_HEAD_DIM
    zero = jnp.zeros((depth, rwkv_w2.shape[1], width), F32)
    wa2 = bf(jnp.stack([jnp.concatenate([rwkv_w2, zero], axis=1), jnp.concatenate([zero, rwkv_a2], axis=1)], axis=1))
    vecs = jnp.stack([rwkv_w0, rwkv_a0, rwkv_k_k, rwkv_k_a, rwkv_r_k.reshape(depth, width), rwkv_ln_w, rwkv_ln_b,
                      jnp.zeros((depth, width), F32)], axis=1)
    mu = rwkv_mu.reshape(depth, 1, -1)
    g2 = bf(rwkv_g2)
    s5w = jax.vmap(_s5_weights)(s5_lam_re, s5_lam_im, s5_log_dt, s5_b_re, s5_b_im, s5_c_re, s5_c_im, s5_d, s5_w_glu)
    wun, wur, wus, wout = bf(w_up_nsa), bf(w_up_rwkv), bf(w_up_s5), bf(w_out)
    wq, wo = bf(xa_w_q), bf(xa_w_o)
    wg, wu, wd = bf(ffn_w_gate), bf(ffn_w_up), bf(ffn_w_down)
    kv_all = _memkv(mem, mem_norm.reshape(depth, 1, d), bf(jnp.concatenate([xa_w_k, xa_w_v], axis=-1)))

    x2 = x.reshape(batch * seq, d)
    for l in range(depth):
        of = lambda a: _Of(a, l)
        gain = lambda i: _Of(gains, l, i)
        z = _inproj(x2, gain(0), cos128, sin128, of(w_all), seq, min(INPROJ_ROW_TILE, seq))

        chunks = lambda a: a.reshape(batch, seq // NSA_CMP_STRIDE, NSA_CMP_STRIDE * a.shape[-1])
        kc2, vc2 = _compress(chunks(z["kc"]), chunks(z["vc"]), *map(of, cmp_k), *map(of, cmp_v))
        y_nsa = _nsa(z, kc2, vc2, batch, seq)
        y_rwkv = _rwkv(z["rw"], of(mu), of(vecs), of(wa2), of(g2), batch, seq)
        y_s5 = _s5(z["s5"], tuple(map(of, s5w)), batch, seq)

        x3 = _layer_tail(x2.reshape(batch, seq, d), y_nsa, y_rwkv, y_s5, z["mg"].reshape(batch, seq, -1), of(kv_all),
                         of(gains), tuple(map(of, (wun, wur, wus, wout, wq, wo, wg, wu, wd))), tm)
        x2 = x3.reshape(batch * seq, d)
    return x2.reshape(batch, seq, d)
```

```python
import functools
import math

import jax
import jax.numpy as jnp
import numpy as np
from jax import lax
from jax.experimental import pallas as pl
from jax.experimental.pallas import tpu as pltpu

NSA_HEADS = 8
NSA_KV_GROUPS = 2
NSA_HEAD_DIM = 64
NSA_CMP_LEN = 32
NSA_CMP_STRIDE = 16
NSA_SEL_LEN = 64
NSA_SEL_TOPK = 16
NSA_WINDOW = 512
RWKV_HEADS = 4
RWKV_HEAD_DIM = 64
RWKV_LN_EPS = 64e-5
S5_GROUPS = 16
S5_GROUP_WIDTH = 16
S5_STATE = 64
XA_HEADS = 4
XA_HEAD_DIM = 64
ROPE_THETA = 10000.0
NORM_EPS = 1e-6
NEG_INF = -1e30
SEL_FORCE_SCORE = 1e9

LANES = 128
SUBLANES = 8
MXU_DTYPE = jnp.bfloat16
VMEM_LIMIT = 56 << 20

F32 = jnp.float32


def _cparams(*sem):
    return pltpu.CompilerParams(dimension_semantics=sem, vmem_limit_bytes=VMEM_LIMIT)


class _Of:
    def __init__(self, arr, *idx):
        self.arr, self.idx = arr, idx

    @property
    def shape(self):
        return self.arr.shape[len(self.idx):]


def _arr(w):
    return w.arr if isinstance(w, _Of) else w


def _spec(w):
    nd = len(w.shape)
    if isinstance(w, _Of):
        lead = tuple(w.idx)
        return pl.BlockSpec((None,) * len(lead) + tuple(w.shape), lambda *_: lead + (0,) * nd,
                            pipeline_mode=pl.Buffered(1))
    return pl.BlockSpec(w.shape, lambda *_: (0,) * nd, pipeline_mode=pl.Buffered(1))


def _dot(a, b):
    return jnp.dot(a.astype(MXU_DTYPE), b.astype(MXU_DTYPE), preferred_element_type=F32)


def _dot_nt(a, b):
    return lax.dot_general(a.astype(MXU_DTYPE), b.astype(MXU_DTYPE), (((1,), (1,)), ((), ())),
                           preferred_element_type=F32)


def _split(a, terms):
    parts = []
    for _ in range(terms - 1):
        p = a.astype(MXU_DTYPE)
        parts.append(p)
        a = a - p.astype(F32)
    parts.append(a.astype(MXU_DTYPE))
    return parts


def _dot_split_lhs(a, b01, terms=3):
    b01 = b01.astype(MXU_DTYPE)
    return sum(jnp.dot(p, b01, preferred_element_type=F32) for p in _split(a, terms))


def _dot_split_rhs(a01, b, terms=3):
    a01 = a01.astype(MXU_DTYPE)
    return sum(jnp.dot(a01, p, preferred_element_type=F32) for p in _split(b, terms))


def _rms(x, g):
    return x * lax.rsqrt(jnp.mean(x * x, axis=-1, keepdims=True) + NORM_EPS) * g


def _sigmoid(x):
    return 1.0 / (1.0 + jnp.exp(-x))


def _gelu(x):
    return 0.5 * x * (1.0 + jnp.tanh(math.sqrt(2.0 / math.pi) * (x + 0.044715 * (x * x * x))))


def _softmax_last(s):
    m = jnp.max(s, axis=-1, keepdims=True)
    e = jnp.exp(s - m)
    return e / jnp.sum(e, axis=-1, keepdims=True)


_SEG = dict(q=(0, 512), kc=(512, 640), vc=(640, 768), ks=(768, 1024), vs=(1024, 1280), kw=(1280, 1536),
            vw=(1536, 1792), ng=(1792, 1920), rw=(1920, 2944), s5=(2944, 3200), mg=(3200, 6272))
_W_IN_COLS = 6272


def _rope128(x, cos, sin_signed):
    lane = lax.broadcasted_iota(jnp.int32, x.shape, 1)
    first = (lane % NSA_HEAD_DIM) < (NSA_HEAD_DIM // 2)
    rot = jnp.where(first, pltpu.roll(x, LANES - NSA_HEAD_DIM // 2, 1), pltpu.roll(x, NSA_HEAD_DIM // 2, 1))
    return x * cos + rot * sin_signed


def _inproj_kernel(x_ref, g_ref, cos_ref, sin_ref, w_ref,
                   qc_o, qr_o, kc_o, vc_o, ks_o, vs_o, kw_o, vw_o, ng_o, rw_o, s5_o, mg_o, *, n_tab):
    h = _rms(x_ref[...], g_ref[...]).astype(MXU_DTYPE)
    cos = cos_ref[...]
    sin = sin_ref[...]
    scale = NSA_HEAD_DIM ** -0.5 * math.log2(math.e)

    def seg(name):
        a, b = _SEG[name]
        return jnp.dot(h, w_ref[:, a:b], preferred_element_type=F32)

    q = seg("q")
    qc_o[...] = (q * scale).astype(qc_o.dtype)
    for c in range(q.shape[1] // LANES):
        sl = slice(c * LANES, (c + 1) * LANES)
        qr_o[:, sl] = (_rope128(q[:, sl], cos, sin) * scale).astype(qr_o.dtype)
    kc_o[...] = seg("kc")
    vc_o[...] = seg("vc")
    k = seg("kw")
    for c in range(NSA_KV_GROUPS):
        sl = slice(c * LANES, (c + 1) * LANES)
        kw_o[:, sl] = _rope128(k[:, sl], cos, sin).astype(kw_o.dtype)
    tm = x_ref.shape[0]
    tok = (pl.program_id(0) % n_tab) * tm + lax.broadcasted_iota(jnp.int32, (tm, LANES), 0)
    onehot = jnp.where(lax.broadcasted_iota(jnp.int32, (tm, LANES), 1) == tok // NSA_SEL_LEN, 1.0, 0.0)
    k = seg("ks")
    for c in range(NSA_KV_GROUPS):
        ks_o[:, 2 * c * LANES:(2 * c + 1) * LANES] = _rope128(k[:, c * LANES:(c + 1) * LANES], cos, sin).astype(ks_o.dtype)
        ks_o[:, (2 * c + 1) * LANES:(2 * c + 2) * LANES] = onehot.astype(ks_o.dtype)
    ones = jnp.ones((tm, LANES), vs_o.dtype)
    for name, out in (("vs", vs_o), ("vw", vw_o)):
        val = seg(name)
        for c in range(NSA_KV_GROUPS):
            out[:, 2 * c * LANES:(2 * c + 1) * LANES] = val[:, c * LANES:(c + 1) * LANES].astype(out.dtype)
            out[:, (2 * c + 1) * LANES:(2 * c + 2) * LANES] = ones
    ng_o[...] = _sigmoid(seg("ng"))
    rw_o[...] = seg("rw")
    s5_o[...] = seg("s5")
    mg_o[...] = _sigmoid(seg("mg")).astype(mg_o.dtype)


def _dup_groups(w):
    hd = NSA_HEAD_DIM
    return jnp.concatenate([w[:, :hd], w[:, :hd], w[:, hd:], w[:, hd:]], axis=1)


def _regroup_w_in(w):
    d = w.shape[0]
    q_w = NSA_HEADS * NSA_HEAD_DIM
    kv_w = NSA_KV_GROUPS * NSA_HEAD_DIM
    o = 0
    wq = w[:, o:o + q_w]; o += q_w
    kv = [w[:, o + i * kv_w:o + (i + 1) * kv_w] for i in range(6)]; o += 6 * kv_w
    wng = w[:, o:o + 3 * NSA_HEADS]; o += 3 * NSA_HEADS
    rw_w = 3 * RWKV_HEADS * RWKV_HEAD_DIM + 256
    wrw = w[:, o:o + rw_w]; o += rw_w
    s5_w = S5_GROUPS * S5_GROUP_WIDTH
    ws5 = w[:, o:o + s5_w]; o += s5_w
    wmg = w[:, o:]
    wng = jnp.pad(wng, ((0, 0), (0, LANES - wng.shape[1])))
    out = jnp.concatenate([wq, kv[0], kv[1], _dup_groups(kv[2]), _dup_groups(kv[3]), _dup_groups(kv[4]),
                           _dup_groups(kv[5]), wng, wrw, ws5, wmg], axis=1)
    assert out.shape == (d, _W_IN_COLS), out.shape
    return out.astype(MXU_DTYPE)


def _inproj(x2, gain, cos128, sin128, w_all, seq, tm):
    m, d = x2.shape
    n_tab = seq // tm
    row = lambda w: pl.BlockSpec((tm, w), lambda i: (i, 0))
    tab = pl.BlockSpec((tm, LANES), lambda i: (i % n_tab, 0))
    widths = dict(qc=512, qr=512, kc=128, vc=128, ks=512, vs=512, kw=256, vw=512, ng=128, rw=1024, s5=256, mg=3072)
    dtypes = dict(qc=MXU_DTYPE, qr=MXU_DTYPE, kc=F32, vc=F32, ks=MXU_DTYPE, vs=MXU_DTYPE, kw=MXU_DTYPE,
                  vw=MXU_DTYPE, ng=F32, rw=F32, s5=F32, mg=MXU_DTYPE)
    names = list(widths)
    outs = pl.pallas_call(
        functools.partial(_inproj_kernel, n_tab=n_tab),
        grid=(m // tm,),
        in_specs=[row(d), _spec(gain), tab, tab, _spec(w_all)],
        out_specs=[row(widths[n]) for n in names],
        out_shape=[jax.ShapeDtypeStruct((m, widths[n]), dtypes[n]) for n in names],
        compiler_params=_cparams("parallel"),
    )(x2, _arr(gain), cos128, sin128, _arr(w_all))
    return dict(zip(names, outs))


def _compress_kernel(k_ref, v_ref, pk_ref, pv_ref, kw1_ref, kw2_ref, vw1_ref, vw2_ref, kc_o, vc_o):
    def one(x_ref, p_ref, w1_ref, w2_ref):
        x = x_ref[0]
        n = x.shape[0]
        top = _dot(x + p_ref[0:1, :], w1_ref[0])
        bot = _dot(x + p_ref[1:2, :], w1_ref[1])
        hid = top + pltpu.roll(bot, n - 1, 0)
        return _dot(_gelu(hid), w2_ref[...])

    kc_o[0] = one(k_ref, pk_ref, kw1_ref, kw2_ref).astype(kc_o.dtype)
    vc = one(v_ref, pv_ref, vw1_ref, vw2_ref)
    for c in range(NSA_KV_GROUPS):
        vc_o[0, :, 2 * c * LANES:(2 * c + 1) * LANES] = vc[:, c * LANES:(c + 1) * LANES].astype(vc_o.dtype)
        vc_o[0, :, (2 * c + 1) * LANES:(2 * c + 2) * LANES] = jnp.ones((vc.shape[0], LANES), vc_o.dtype)


def _compress_weights(pos, w1, w2):
    g, hd, half = NSA_KV_GROUPS, NSA_HEAD_DIM, NSA_CMP_STRIDE
    hidden = w1.shape[1]
    eye = jnp.eye(g, dtype=w1.dtype)
    w1r = w1.reshape(2, half, hd, hidden)
    w1e = jnp.einsum("pldj,ab->pladbj", w1r, eye).reshape(2, half * g * hd, g * hidden)
    w2d = jnp.concatenate([w2, w2], axis=1)
    w2e = jnp.einsum("jd,ab->ajbd", w2d, eye).reshape(g * hidden, g * 2 * hd)
    pos_e = jnp.broadcast_to(pos.reshape(2, half, 1, hd), (2, half, g, hd)).reshape(2, half * g * hd)
    return pos_e.astype(F32), w1e.astype(MXU_DTYPE), w2e.astype(MXU_DTYPE)


def _compress(kc, vc, pk, kw1, kw2, pv, vw1, vw2):
    b, n, width = kc.shape
    blk = pl.BlockSpec((1, n, width), lambda i: (i, 0, 0))
    out_w = kw2.shape[1]
    out_blk = lambda w: pl.BlockSpec((1, n, w), lambda i: (i, 0, 0))
    return pl.pallas_call(
        _compress_kernel,
        grid=(b,),
        in_specs=[blk, blk] + [_spec(w) for w in (pk, pv, kw1, kw2, vw1, vw2)],
        out_specs=[out_blk(out_w), out_blk(2 * out_w)],
        out_shape=[jax.ShapeDtypeStruct((b, n, out_w), MXU_DTYPE), jax.ShapeDtypeStruct((b, n, 2 * out_w), MXU_DTYPE)],
        compiler_params=_cparams("parallel"),
    )(kc, vc, *map(_arr, (pk, pv, kw1, kw2, vw1, vw2)))


NSA_TQ = 128
NSA_TK = 1024
MASKED = -1e30


def _nsa_kernel(qc_ref, qr_ref, kc_ref, vc_ref, ks_ref, vs_ref, kw_ref, vw_ref, ng_ref, ovl_ref, gexp_ref, y_ref, *, top_k):
    tq = qc_ref.shape[1]
    seq = ks_ref.shape[1]
    n_cmp_pad = kc_ref.shape[1]
    hpg = NSA_HEADS // NSA_KV_GROUPS
    rows = hpg * tq
    n_sel = seq // NSA_SEL_LEN
    n_sel_pad = -(-n_sel // SUBLANES) * SUBLANES
    t0 = pl.program_id(1) * tq
    tcol = t0 + lax.broadcasted_iota(jnp.int32, (tq, 1), 0)
    low_half = lax.broadcasted_iota(jnp.int32, (tq, LANES), 1) < NSA_HEAD_DIM

    def stack_heads(q_ref, g):
        parts = []
        for r in range(hpg):
            h = g * hpg + r
            slab = q_ref[0, :, (h // 2) * LANES:(h // 2 + 1) * LANES]
            keep = low_half if h % 2 == 0 else jnp.logical_not(low_half)
            parts.append(jnp.where(keep, slab, jnp.zeros_like(slab)))
        return jnp.concatenate(parts, axis=0)

    def unstack(o, g, y_parts):
        for pair in range(hpg // 2):
            even = o[(2 * pair) * tq:(2 * pair + 1) * tq]
            odd = o[(2 * pair + 1) * tq:(2 * pair + 2) * tq]
            y_parts[g * (hpg // 2) + pair] = jnp.where(low_half, even, odd)

    def masked_exp_pv(q, k, bias, v_ones):
        s = [_dot_nt(q[g], k[g]) for g in groups]
        n = s[0].shape[-1]
        s = [sg.reshape(hpg, tq, n) + bias[None] for sg in s]
        e = [jnp.exp2(sg - jnp.max(sg, axis=-1, keepdims=True)) for sg in s]
        return e, [_dot(e[g].reshape(rows, n), v_ones[g]) for g in groups]

    cmp_end = lax.broadcasted_iota(jnp.int32, (1, n_cmp_pad), 1) * NSA_CMP_STRIDE + (NSA_CMP_LEN - 1)
    cmp_bias = jnp.where(cmp_end <= tcol, 0.0, NEG_INF)
    cmp_live = jnp.where(tcol >= NSA_CMP_LEN - 1, 1.0, 0.0)
    slab = NSA_WINDOW + tq
    w0 = pl.multiple_of(jnp.maximum(t0 - NSA_WINDOW, 0), tq)
    wpos = w0 + lax.broadcasted_iota(jnp.int32, (1, slab), 1)
    win_bias = jnp.where(wpos <= tcol, jnp.where(wpos > tcol - NSA_WINDOW, 0.0, NEG_INF), NEG_INF)
    n_kt = (t0 + tq - 1) // NSA_TK + 1
    k_last = pl.multiple_of((n_kt - 1) * NSA_TK, NSA_TK)

    blk = lax.broadcasted_iota(jnp.int32, (n_sel_pad, tq), 0)
    cur = (t0 + lax.broadcasted_iota(jnp.int32, (n_sel_pad, tq), 1)) // NSA_SEL_LEN
    forced = (blk == 0) | (blk == cur) | (blk == cur - 1)

    groups = range(NSA_KV_GROUPS)
    gl = [slice(g * LANES, (g + 1) * LANES) for g in groups]
    gx = [slice(2 * g * LANES, (2 * g + 2) * LANES) for g in groups]
    y_cmp, y_sel, y_win = [None] * 4, [None] * 4, [None] * 4
    qc = [stack_heads(qc_ref, g) for g in groups]
    qr = [stack_heads(qr_ref, g) for g in groups]

    cmp_e, cmp_pv = masked_exp_pv(qc, [kc_ref[0, :, gl[g]] for g in groups], cmp_bias, [vc_ref[0, :, gx[g]] for g in groups])
    cmp_inv = [(cmp_live[None] / cmp_pv[g][:, LANES:].reshape(hpg, tq, LANES)) for g in groups]
    for g in groups:
        unstack(cmp_pv[g][:, :LANES] * cmp_inv[g].reshape(rows, LANES), g, y_cmp)

    _, win = masked_exp_pv(qr, [kw_ref[0, pl.ds(w0, slab), gl[g]] for g in groups], win_bias,
                           [vw_ref[0, pl.ds(w0, slab), gx[g]] for g in groups])
    for g in groups:
        unstack(win[g][:, :LANES] / win[g][:, LANES:], g, y_win)

    qx = []
    for g in groups:
        p = cmp_e[g] * jnp.concatenate([cmp_inv[g]] * -(-n_cmp_pad // LANES), axis=-1)[..., :n_cmp_pad]
        imp = _dot_split_lhs(jnp.sum(p, axis=0), ovl_ref[...])
        vals = jnp.where(blk <= cur, jnp.where(forced, SEL_FORCE_SCORE, imp.T[:n_sel_pad]), NEG_INF)
        rank = [jnp.zeros((SUBLANES, tq), F32) for _ in range(n_sel_pad // SUBLANES)]
        for i in range(n_sel):
            other = vals[i:i + 1, :]
            for u in range(n_sel_pad // SUBLANES):
                mine = vals[u * SUBLANES:(u + 1) * SUBLANES]
                if i < u * SUBLANES:
                    ahead = other >= mine
                elif i >= (u + 1) * SUBLANES:
                    ahead = other > mine
                else:
                    ahead = (other > mine) | ((other == mine) & (blk[u * SUBLANES:(u + 1) * SUBLANES] > i))
                rank[u] = rank[u] + jnp.where(ahead, 1.0, 0.0)
        drop = jnp.where(jnp.concatenate(rank, axis=0) < top_k, 0.0, MASKED)
        drop = jnp.concatenate([drop, jnp.zeros((LANES - n_sel_pad, tq), F32)], axis=0).T.astype(MXU_DTYPE)
        qx.append(jnp.concatenate([qr[g], jnp.concatenate([drop] * hpg, axis=0)], axis=1))

    def sel_tiles(k0, tk, causal, carries):
        s = [_dot_nt(qx[g], ks_ref[0, pl.ds(k0, tk), gx[g]]).reshape(hpg, tq, tk) for g in groups]
        if causal:
            bias = jnp.where(k0 + lax.broadcasted_iota(jnp.int32, (1, tk), 1) <= tcol, 0.0, NEG_INF)
            s = [sg + bias[None] for sg in s]
        m_new = [jnp.maximum(carries[g][0], jnp.max(s[g], axis=-1, keepdims=True)) for g in groups]
        e = [jnp.exp2(s[g] - m_new[g]) for g in groups]
        pv = [_dot(e[g].reshape(rows, tk), vs_ref[0, pl.ds(k0, tk), gx[g]]) for g in groups]
        return tuple((m_new[g], jnp.exp2(carries[g][0] - m_new[g]) * carries[g][1] + pv[g].reshape(hpg, tq, 2 * LANES))
                     for g in groups)

    init = (jnp.full((hpg, tq, 1), NEG_INF, F32), jnp.zeros((hpg, tq, 2 * LANES), F32))
    carries = lax.fori_loop(0, n_kt - 1, lambda j, c: sel_tiles(pl.multiple_of(j * NSA_TK, NSA_TK), NSA_TK, False, c),
                            tuple(init for _ in groups))
    carries = sel_tiles(k_last, NSA_TK, True, carries)
    for g in groups:
        acc = carries[g][1].reshape(rows, 2 * LANES)
        unstack(acc[:, :LANES] / acc[:, LANES:], g, y_sel)

    gates = ng_ref[0]
    q_w = NSA_HEADS * NSA_HEAD_DIM
    spread = _dot_split_lhs(gates, gexp_ref[...], terms=2)
    for c in range(4):
        sl = slice(c * LANES, (c + 1) * LANES)
        y_ref[0, :, sl] = (spread[:, sl] * y_cmp[c] + spread[:, q_w + c * LANES:q_w + (c + 1) * LANES] * y_sel[c]
                           + spread[:, 2 * q_w + c * LANES:2 * q_w + (c + 1) * LANES] * y_win[c]).astype(y_ref.dtype)


def _nsa_constants(seq, n_cmp_pad):
    n_sel = seq // NSA_SEL_LEN
    cmp_start = np.arange(n_cmp_pad) * NSA_CMP_STRIDE
    sel_start = np.arange(LANES) * NSA_SEL_LEN
    ovl = ((cmp_start[:, None] < sel_start[None, :] + NSA_SEL_LEN) & (cmp_start[:, None] + NSA_CMP_LEN > sel_start[None, :])
           & (np.arange(LANES)[None, :] < n_sel) & (np.arange(n_cmp_pad)[:, None] < n_cmp_pad - 1))
    q_w = NSA_HEADS * NSA_HEAD_DIM
    gexp = np.zeros((LANES, 3 * q_w), np.float32)
    for i in range(3):
        for h in range(NSA_HEADS):
            gexp[3 * h + i, i * q_w + h * NSA_HEAD_DIM:i * q_w + (h + 1) * NSA_HEAD_DIM] = 1.0
    return jnp.asarray(ovl.astype(np.float32), dtype=MXU_DTYPE), jnp.asarray(gexp, dtype=MXU_DTYPE)


def _nsa(z, kc2, vc2, batch, seq):
    tq = NSA_TQ
    q_w = NSA_HEADS * NSA_HEAD_DIM
    n_cmp_pad = kc2.shape[1]
    ovl, gexp = _nsa_constants(seq, n_cmp_pad)
    r3 = lambda a: a.reshape(batch, seq, a.shape[-1])
    qblk = pl.BlockSpec((1, tq, q_w), lambda b, i: (b, i, 0))
    full = lambda n, w: pl.BlockSpec((1, n, w), lambda b, i: (b, 0, 0))
    top_k = min(NSA_SEL_TOPK, seq // NSA_SEL_LEN)
    return pl.pallas_call(
        functools.partial(_nsa_kernel, top_k=top_k),
        grid=(batch, seq // tq),
        in_specs=[qblk, qblk, full(n_cmp_pad, 256), full(n_cmp_pad, 512), full(seq, 512), full(seq, 512),
                  full(seq, 256), full(seq, 512), pl.BlockSpec((1, tq, LANES), lambda b, i: (b, i, 0)),
                  _spec(ovl), _spec(gexp)],
        out_specs=qblk,
        out_shape=jax.ShapeDtypeStruct((batch, seq, q_w), MXU_DTYPE),
        compiler_params=_cparams("parallel", "arbitrary"),
    )(r3(z["qc"]), r3(z["qr"]), kc2, vc2, r3(z["ks"]), r3(z["vs"]), r3(z["kw"]), r3(z["vw"]), r3(z["ng"]),
      ovl, gexp)


RWKV_CHUNK = 64
RWKV_TB = 512


def _rwkv_kernel(z_ref, mu_ref, vecs_ref, wa2_ref, g2_ref, ones_ref, o_ref, prev_ref, st_ref):
    tb = z_ref.shape[1]
    width = RWKV_HEADS * RWKV_HEAD_DIM
    n = RWKV_HEAD_DIM
    c = RWKV_CHUNK

    @pl.when(pl.program_id(1) == 0)
    def _():
        prev_ref[...] = jnp.zeros_like(prev_ref)
        st_ref[...] = jnp.zeros_like(st_ref)

    z = z_ref[0]
    row = lax.broadcasted_iota(jnp.int32, (tb, 1), 0)
    z_prev = jnp.where(row == 0, prev_ref[...], pltpu.roll(z, 1, 0))
    prev_ref[...] = z[tb - 1:tb, :]
    zs = z + (z_prev - z) * mu_ref[...]

    w0, a0, k_k, k_a, r_k, ln_w, ln_b = (vecs_ref[i:i + 1, :] for i in range(7))
    r = zs[:, 0:width]
    k = zs[:, width:2 * width]
    v = zs[:, 2 * width:3 * width]
    lr = zs[:, 3 * width:3 * width + LANES]
    w = w0 + _dot(jnp.tanh(lr), wa2_ref[0])
    w = -(jnp.maximum(-w, 0.0) + jnp.log(1.0 + jnp.exp(-jnp.abs(w)))) - 0.5
    logd = -jnp.exp(w)
    a = _sigmoid(a0 + _dot(lr, wa2_ref[1]))
    gate = _dot(_sigmoid(zs[:, 3 * width + LANES:]), g2_ref[...])
    ones_h = ones_ref[...]
    kk = k * k_k
    kk = kk / jnp.maximum(jnp.sqrt(_dot_split_lhs(kk * kk, ones_h, terms=2)), 1e-12)
    k = k * (1.0 + (a - 1.0) * k_a)
    bonus = _dot_split_lhs(r * k * r_k, ones_h, terms=2) * v

    ri = lax.broadcasted_iota(jnp.int32, (tb, tb), 0)
    ci = lax.broadcasted_iota(jnp.int32, (tb, tb), 1)
    same = (ri // c) == (ci // c)
    cl = _dot_split_rhs(jnp.where(same & (ci <= ri), 1.0, 0.0), logd)
    cend = jnp.concatenate([jnp.broadcast_to(cl[q * c + c - 1:q * c + c, :], (c, width)) for q in range(tb // c)],
                           axis=0)
    e_inv = jnp.exp(-cl)
    e_end = jnp.exp(cend - cl)
    a_t = -kk * jnp.exp(cl - logd)
    r_t = r * jnp.exp(cl)
    b = kk * a
    b_t = b * e_inv
    k_t = k * e_inv
    b_end_T = (b * e_end).T
    k_end_T = (k * e_end).T
    w_end = jnp.exp(cend)

    head_of_row = lax.broadcasted_iota(jnp.int32, (width, width), 0) // n
    head_of_lane = lax.broadcasted_iota(jnp.int32, (width, width), 1) // n
    on_diag_block = head_of_row == head_of_lane
    eye_w = lax.broadcasted_iota(jnp.int32, (width, width), 0) == lax.broadcasted_iota(jnp.int32, (width, width), 1)
    step = lax.broadcasted_iota(jnp.int32, (c, width), 0)
    within = lax.broadcasted_iota(jnp.int32, (c, width), 1) % n
    strict = within < step
    incl = within <= step
    eye_c = jnp.where(within == step, 1.0, 0.0)

    def bd(m):
        m = m.astype(MXU_DTYPE)
        return jnp.where(on_diag_block, jnp.concatenate([m] * RWKV_HEADS, axis=0), jnp.zeros((), MXU_DTYPE))

    def chunk_maps(chunk_ids):
        qs = [slice(q * c, (q + 1) * c) for q in chunk_ids]
        ids = range(len(qs))
        xs = [jnp.concatenate([a_t[s], r_t[s]], axis=0) for s in qs]
        pb = [_dot_nt(xs[i], bd(b_t[qs[i]])) for i in ids]
        pk = [_dot_nt(xs[i], bd(k_t[qs[i]])) for i in ids]
        n_low = [jnp.where(strict, p[:c], 0.0) for p in pb]
        t_inv = [eye_c + jnp.where((step // 2) == (within // 2), m, 0.0) for m in n_low]
        size = 2
        while size < c:
            off = ((step // (2 * size)) == (within // (2 * size))) & ((step // size) != (within // size))
            y = [_dot(jnp.where(off, m, 0.0), bd(t)) for m, t in zip(n_low, t_inv)]
            t_inv = [t + _dot(t, bd(yy)) for t, yy in zip(t_inv, y)]
            size *= 2
        v_bd = [bd(v[s]) for s in qs]
        nv = [_dot(jnp.where(strict, pk[i][:c], 0.0), v_bd[i]) for i in ids]
        mg1 = [_dot(t_inv[i], jnp.concatenate([bd(a_t[qs[i]]), bd(nv[i])], axis=1)) for i in ids]
        dm = [_dot(jnp.where(incl, pb[i][c:], 0.0), jnp.concatenate([bd(mg1[i][:, :width]), bd(mg1[i][:, width:])], axis=1))
              for i in ids]
        drk_v = [_dot(jnp.where(incl, pk[i][c:], 0.0), v_bd[i]) for i in ids]
        bm = [_dot(b_end_T[:, qs[i]], mg1[i]) for i in ids]
        kv_end = [_dot(k_end_T[:, qs[i]], v[qs[i]]) for i in ids]
        maps = []
        for i, q in enumerate(chunk_ids):
            m2 = r_t[qs[i]] + dm[i][:, :width]
            g2 = dm[i][:, width:] + drk_v[i]
            m3 = jnp.where(on_diag_block, bm[i][:, :width], 0.0) + jnp.where(eye_w, w_end[q * c:q * c + 1, :], 0.0)
            g3 = jnp.where(on_diag_block, bm[i][:, width:] + kv_end[i], 0.0)
            maps.append((m2, g2, m3, g3))
        return maps

    st = st_ref[...]
    ys = []
    for m2, g2, m3, g3 in chunk_maps(range(tb // c)):
        ys.append(_dot(m2, st) + g2)
        st = _dot(m3, st) + g3
    st_ref[...] = st
    y = jnp.concatenate(ys, axis=0)
    inv_n = 1.0 / n
    mean = _dot_split_lhs(y, ones_h, terms=2) * inv_n
    var = _dot_split_lhs(jnp.square(y - mean), ones_h, terms=2) * inv_n
    yn = (y - mean) * lax.rsqrt(var + RWKV_LN_EPS)
    o_ref[0] = (((yn * ln_w + ln_b) + bonus) * gate).astype(o_ref.dtype)


def _rwkv(z_rw, mu, vecs, wa2, g2, batch, seq):
    tb = min(RWKV_TB, seq)
    width = RWKV_HEADS * RWKV_HEAD_DIM
    head = np.arange(width) // RWKV_HEAD_DIM
    ones_h = jnp.asarray((head[:, None] == head[None, :]).astype(np.float32))
    z3 = z_rw.reshape(batch, seq, z_rw.shape[-1])
    return pl.pallas_call(
        _rwkv_kernel,
        grid=(batch, seq // tb),
        in_specs=[pl.BlockSpec((1, tb, z3.shape[-1]), lambda b, i: (b, i, 0))]
        + [_spec(w) for w in (mu, vecs, wa2, g2, ones_h)],
        out_specs=pl.BlockSpec((1, tb, width), lambda b, i: (b, i, 0)),
        out_shape=jax.ShapeDtypeStruct((batch, seq, width), MXU_DTYPE),
        scratch_shapes=[pltpu.VMEM((1, z3.shape[-1]), F32), pltpu.VMEM((width, width), F32)],
        compiler_params=_cparams("parallel", "arbitrary"),
    )(z3, *map(_arr, (mu, vecs, wa2, g2, ones_h)))


S5_TB = 512


def _s5_kernel(u_ref, perm_ref, win_ref, abar_ref, pow_ref, cout_ref, d_ref, wglu_ref, o_ref, xr_ref, xi_ref, carry_ref):
    tb = u_ref.shape[1]
    ns = S5_GROUPS * S5_STATE
    steps = tb // SUBLANES

    @pl.when(pl.program_id(1) == 0)
    def _():
        carry_ref[...] = jnp.zeros_like(carry_ref)

    u = u_ref[0]
    u_perm = jnp.dot(perm_ref[0], u.astype(MXU_DTYPE), preferred_element_type=F32)
    xr_ref[...] = _dot(u_perm, win_ref[:, :ns])
    xi_ref[...] = _dot(u_perm, win_ref[:, ns:])
    ar, ai = abar_ref[0:1, :], abar_ref[1:2, :]

    def scan_step(s, state):
        sr, si = state
        rows = pl.ds(pl.multiple_of(s * SUBLANES, SUBLANES), SUBLANES)
        nr = ar * sr - ai * si + xr_ref[rows, :]
        ni = ar * si + ai * sr + xi_ref[rows, :]
        xr_ref[rows, :] = nr
        xi_ref[rows, :] = ni
        return nr, ni

    zero = jnp.zeros((SUBLANES, ns), F32)
    fr, fi = lax.fori_loop(0, steps, scan_step, (zero, zero), unroll=4)

    cr, ci = abar_ref[2:3, :], abar_ref[3:4, :]
    er, ei = [carry_ref[0:1, :]], [carry_ref[1:2, :]]
    for c in range(SUBLANES):
        pr, pi = er[-1], ei[-1]
        er.append(fr[c:c + 1, :] + (cr * pr - ci * pi))
        ei.append(fi[c:c + 1, :] + (cr * pi + ci * pr))
    carry_ref[0:1, :] = er[SUBLANES]
    carry_ref[1:2, :] = ei[SUBLANES]
    enter_r = jnp.concatenate(er[:SUBLANES], axis=0)
    enter_i = jnp.concatenate(ei[:SUBLANES], axis=0)

    def add_entering(s, _):
        rows = pl.ds(pl.multiple_of(s * SUBLANES, SUBLANES), SUBLANES)
        pr, pi = pow_ref[0, pl.ds(s, 1), :], pow_ref[1, pl.ds(s, 1), :]
        xr_ref[rows, :] = xr_ref[rows, :] + (pr * enter_r - pi * enter_i)
        xi_ref[rows, :] = xi_ref[rows, :] + (pr * enter_i + pi * enter_r)
        return 0

    lax.fori_loop(0, steps, add_entering, 0, unroll=4)
    y_perm = _dot(xr_ref[...], cout_ref[0]) + _dot(xi_ref[...], cout_ref[1])
    y = _gelu(_dot_split_rhs(perm_ref[1], y_perm, terms=2) + d_ref[...] * u)
    o_ref[0] = (y * _sigmoid(_dot(y, wglu_ref[...]))).astype(o_ref.dtype)


def _s5_weights(lam_re, lam_im, log_dt, b_re, b_im, c_re, c_im, d, w_glu):
    g, p, hw = S5_GROUPS, S5_STATE, S5_GROUP_WIDTH
    dt = jnp.exp(log_dt.astype(F32))[:, None]
    lr, li = lam_re.astype(F32), lam_im.astype(F32)
    mag = jnp.exp(lr * dt)
    ar, ai = mag * jnp.cos(li * dt), mag * jnp.sin(li * dt)
    den = lr * lr + li * li
    cr = ((ar - 1.0) * lr + ai * li) / den
    ci = (ai * lr - (ar - 1.0) * li) / den
    eye = jnp.eye(g, dtype=F32)
    w_re = cr[:, :, None] * b_re - ci[:, :, None] * b_im
    w_im = cr[:, :, None] * b_im + ci[:, :, None] * b_re
    bd_in = lambda w: jnp.einsum("gph,ga->ghap", w, eye).reshape(g * hw, g * p)
    win = jnp.concatenate([bd_in(w_re), bd_in(w_im)], axis=1).astype(MXU_DTYPE)
    bd_out = lambda w: jnp.einsum("ghp,ga->gpah", w, eye).reshape(g * p, g * hw)
    cout = jnp.stack([bd_out(c_re.astype(F32)), -bd_out(c_im.astype(F32))]).astype(MXU_DTYPE)
    pr, pi = ar.reshape(1, g * p), ai.reshape(1, g * p)
    steps = S5_TB // SUBLANES
    while pr.shape[0] < steps:
        tr, ti = pr[-1:], pi[-1:]
        pr, pi = (jnp.concatenate([pr, pr * tr - pi * ti], axis=0), jnp.concatenate([pi, pr * ti + pi * tr], axis=0))
    powers = jnp.stack([pr[:steps], pi[:steps]])
    abar = jnp.concatenate([pr[0:1], pi[0:1], pr[steps - 1:steps], pi[steps - 1:steps]], axis=0)
    return win, abar, powers, cout, d.reshape(1, g * hw).astype(F32), w_glu.astype(MXU_DTYPE)


def _s5(u, weights, batch, seq):
    win, abar, powers, cout, d, wglu = weights
    tb = S5_TB
    assert seq % tb == 0
    ns = S5_GROUPS * S5_STATE
    width = S5_GROUPS * S5_GROUP_WIDTH
    steps = tb // SUBLANES
    src = (np.arange(tb) % SUBLANES) * steps + np.arange(tb) // SUBLANES
    fwd = (src[:, None] == np.arange(tb)[None, :]).astype(np.float32)
    perm = jnp.asarray(np.stack([fwd, fwd.T]), dtype=MXU_DTYPE)
    u3 = u.reshape(batch, seq, width)
    blk = pl.BlockSpec((1, tb, width), lambda b, i: (b, i, 0))
    return pl.pallas_call(
        _s5_kernel,
        grid=(batch, seq // tb),
        in_specs=[blk] + [_spec(w) for w in (perm, win, abar, powers, cout, d, wglu)],
        out_specs=blk,
        out_shape=jax.ShapeDtypeStruct((batch, seq, width), MXU_DTYPE),
        scratch_shapes=[pltpu.VMEM((tb, ns), F32), pltpu.VMEM((tb, ns), F32), pltpu.VMEM((SUBLANES, ns), F32)],
        compiler_params=_cparams("parallel", "arbitrary"),
    )(u3, *map(_arr, (perm, win, abar, powers, cout, d, wglu)))


def _memkv_kernel(m_ref, g_ref, w_ref, o_ref):
    o_ref[0] = _dot(_rms(m_ref[0], g_ref[...]), w_ref[...]).astype(o_ref.dtype)


def _memkv(mem, gain, wkv):
    b, n, d = mem.shape
    depth, _, w2 = wkv.shape
    return pl.pallas_call(
        _memkv_kernel,
        grid=(depth, b),
        in_specs=[pl.BlockSpec((1, n, d), lambda l, i: (i, 0, 0)), pl.BlockSpec((None, 1, d), lambda l, i: (l, 0, 0)),
                  pl.BlockSpec((None, d, w2), lambda l, i: (l, 0, 0))],
        out_specs=pl.BlockSpec((None, 1, n, w2), lambda l, i: (l, i, 0, 0)),
        out_shape=jax.ShapeDtypeStruct((depth, b, n, w2), MXU_DTYPE),
        compiler_params=_cparams("parallel", "parallel"),
    )(mem, gain, wkv)


FFN_CHUNK = 256


def _layer_tail_kernel(x_ref, yn_ref, yr_ref, ys_ref, mg_ref, kv_ref, gains_ref, wun_ref, wur_ref, wus_ref, wout_ref,
                       wq_ref, wo_ref, wg_ref, wu_ref, wd_ref, o_ref):
    x = x_ref[0]
    d = x.shape[1]
    gain = lambda i: gains_ref[i]

    merged = (mg_ref[0, :, 0:d] * _dot(yn_ref[0], wun_ref[...]) + mg_ref[0, :, d:2 * d] * _dot(yr_ref[0], wur_ref[...])
              + mg_ref[0, :, 2 * d:3 * d] * _dot(ys_ref[0], wus_ref[...]))
    x = x + _rms(_dot(merged, wout_ref[...]), gain(1))

    width = XA_HEADS * XA_HEAD_DIM
    q = (_dot(_rms(x, gain(2)), wq_ref[...]) * (XA_HEAD_DIM ** -0.5)).astype(MXU_DTYPE)
    k = kv_ref[0, :, 0:width]
    v = kv_ref[0, :, width:2 * width]
    head = lax.broadcasted_iota(jnp.int32, q.shape, 1) // XA_HEAD_DIM
    att = jnp.zeros(q.shape, F32)
    for h in range(XA_HEADS):
        mine = head == h
        p = _softmax_last(_dot_nt(jnp.where(mine, q, jnp.zeros_like(q)), k))
        att = att + jnp.where(mine, _dot(p, v), 0.0)
    x = x + _rms(_dot(att, wo_ref[...]), gain(3))

    h = _rms(x, gain(4)).astype(MXU_DTYPE)
    acc = jnp.zeros(x.shape, F32)
    for c in range(wg_ref.shape[1] // FFN_CHUNK):
        sl = slice(c * FFN_CHUNK, (c + 1) * FFN_CHUNK)
        a = jnp.dot(h, wg_ref[:, sl], preferred_element_type=F32)
        b = jnp.dot(h, wu_ref[:, sl], preferred_element_type=F32)
        acc = acc + _dot(a * _sigmoid(a) * b, wd_ref[sl, :])
    o_ref[0] = x + _rms(acc, gain(5))


def _layer_tail(x3, y_nsa, y_rwkv, y_s5, mg, kv, gains, weights, tm):
    b, s, d = x3.shape
    assert weights[-1].shape[0] % FFN_CHUNK == 0
    blk = lambda a: pl.BlockSpec((1, tm, a.shape[-1]), lambda i, j: (i, j, 0))
    kv_spec = pl.BlockSpec((None, 1) + tuple(kv.shape[1:]), lambda i, j: tuple(kv.idx) + (i, 0, 0))
    return pl.pallas_call(
        _layer_tail_kernel,
        grid=(b, s // tm),
        in_specs=[blk(x3), blk(y_nsa), blk(y_rwkv), blk(y_s5), blk(mg), kv_spec, _spec(gains)] + [_spec(w) for w in weights],
        out_specs=blk(x3),
        out_shape=jax.ShapeDtypeStruct(x3.shape, F32),
        compiler_params=_cparams("parallel", "parallel"),
    )(x3, y_nsa, y_rwkv, y_s5, mg, *map(_arr, (kv, gains) + tuple(weights)))


ROW_TILE = 512
INPROJ_ROW_TILE = 256


def kernel(x, mem, norm_gains, mem_norm, w_in, nsa_cmp_pos_k, nsa_cmp_pos_v, nsa_ck_w1, nsa_ck_w2, nsa_cv_w1, nsa_cv_w2, rwkv_mu, rwkv_w0, rwkv_w2, rwkv_a0, rwkv_a2, rwkv_g2, rwkv_k_k, rwkv_k_a, rwkv_r_k, rwkv_ln_w, rwkv_ln_b, s5_lam_re, s5_lam_im, s5_log_dt, s5_b_re, s5_b_im, s5_c_re, s5_c_im, s5_d, s5_w_glu, w_up_nsa, w_up_rwkv, w_up_s5, w_out, xa_w_q, xa_w_k, xa_w_v, xa_w_o, ffn_w_gate, ffn_w_up, ffn_w_down):
    batch, seq, d = x.shape
    depth = w_in.shape[0]
    tm = min(ROW_TILE, seq)
    bf = lambda a: a.astype(MXU_DTYPE)

    half = NSA_HEAD_DIM // 2
    inv = 1.0 / (ROPE_THETA ** (jnp.arange(0, NSA_HEAD_DIM, 2, dtype=F32) / NSA_HEAD_DIM))
    ang = jnp.arange(seq, dtype=F32)[:, None] * inv[None, :]
    cos, sin = jnp.cos(ang), jnp.sin(ang)
    cos128 = jnp.tile(cos, (1, LANES // half))
    sin128 = jnp.tile(jnp.concatenate([-sin, sin], axis=1), (1, LANES // NSA_HEAD_DIM))

    gains = norm_gains.reshape(depth, norm_gains.shape[1], 1, d)
    w_all = jax.vmap(_regroup_w_in)(w_in)
    cmp_k = jax.vmap(_compress_weights)(nsa_cmp_pos_k, nsa_ck_w1, nsa_ck_w2)
    cmp_v = jax.vmap(_compress_weights)(nsa_cmp_pos_v, nsa_cv_w1, nsa_cv_w2)
    width = RWKV_HEADS * RWKV_HEAD_DIM
    zero = jnp.zeros((depth, rwkv_w2.shape[1], width), F32)
    wa2 = bf(jnp.stack([jnp.concatenate([rwkv_w2, zero], axis=1), jnp.concatenate([zero, rwkv_a2], axis=1)], axis=1))
    vecs = jnp.stack([rwkv_w0, rwkv_a0, rwkv_k_k, rwkv_k_a, rwkv_r_k.reshape(depth, width), rwkv_ln_w, rwkv_ln_b,
                      jnp.zeros((depth, width), F32)], axis=1)
    mu = rwkv_mu.reshape(depth, 1, -1)
    g2 = bf(rwkv_g2)
    s5w = jax.vmap(_s5_weights)(s5_lam_re, s5_lam_im, s5_log_dt, s5_b_re, s5_b_im, s5_c_re, s5_c_im, s5_d, s5_w_glu)
    wun, wur, wus, wout = bf(w_up_nsa), bf(w_up_rwkv), bf(w_up_s5), bf(w_out)
    wq, wo = bf(xa_w_q), bf(xa_w_o)
    wg, wu, wd = bf(ffn_w_gate), bf(ffn_w_up), bf(ffn_w_down)
    kv_all = _memkv(mem, mem_norm.reshape(depth, 1, d), bf(jnp.concatenate([xa_w_k, xa_w_v], axis=-1)))

    x2 = x.reshape(batch * seq, d)
    for l in range(depth):
        of = lambda a: _Of(a, l)
        gain = lambda i: _Of(gains, l, i)
        z = _inproj(x2, gain(0), cos128, sin128, of(w_all), seq, min(INPROJ_ROW_TILE, seq))

        chunks = lambda a: a.reshape(batch, seq // NSA_CMP_STRIDE, NSA_CMP_STRIDE * a.shape[-1])
        kc2, vc2 = _compress(chunks(z["kc"]), chunks(z["vc"]), *map(of, cmp_k), *map(of, cmp_v))
        y_nsa = _nsa(z, kc2, vc2, batch, seq)
        y_rwkv = _rwkv(z["rw"], of(mu), of(vecs), of(wa2), of(g2), batch, seq)
        y_s5 = _s5(z["s5"], tuple(map(of, s5w)), batch, seq)

        x3 = _layer_tail(x2.reshape(batch, seq, d), y_nsa, y_rwkv, y_s5, z["mg"].reshape(batch, seq, -1), of(kv_all),
                         of(gains), tuple(map(of, (wun, wur, wus, wout, wq, wo, wg, wu, wd))), tm)
        x2 = x3.reshape(batch * seq, d)
    return x2.reshape(batch, seq, d)
```
